```python
import math
import jax, jax.numpy as jnp
from jax import lax
import numpy as np

D_MODEL = 1024
BATCH = 8
SEQ = 8192
DEPTH = 2
DEC_BATCH = 32
DEC_SEQ = 2048
PAST_LEN = 128

HEAD_DIM = 64
N_HEADS = D_MODEL // HEAD_DIM
NA_HEADS = N_HEADS // 4
DIFF_HEADS = N_HEADS // 4
DIL_HEADS = N_HEADS - NA_HEADS - DIFF_HEADS
NA_WIDTH = NA_HEADS * HEAD_DIM
DIFF_WIDTH = DIFF_HEADS * HEAD_DIM
DIL_WIDTH = DIL_HEADS * HEAD_DIM
MIX_WIDTH = NA_WIDTH + DIFF_WIDTH + DIL_WIDTH
IN_WIDTH = 3 * MIX_WIDTH
GRID_W = 64
NA_WIN_ROWS = 8
NA_WIN_COLS = 16
DIFF_QK_DIM = HEAD_DIM // 2
DIFF_BLOCK = 128
DIL_PATTERNS = ((128, 1), (512, 4), (2048, 16))
FFN_HIDDEN = -(-8 * D_MODEL // (3 * 256)) * 256
ROPE_THETA = 10000.0
LN_EPS = 1e-5
DEEPNORM_ALPHA = (2 * DEPTH) ** 0.25
DEEPNORM_BETA = (8 * DEPTH) ** -0.25
NEG_INF = -1e30

kernel_name = "hybrid_natten_diff_dilated_encoder"


def layer_norm(x, g, b):
    xf = x.astype(jnp.float32)
    mu = jnp.mean(xf, axis=-1, keepdims=True)
    var = jnp.mean(jnp.square(xf - mu), axis=-1, keepdims=True)
    return ((xf - mu) * lax.rsqrt(var + LN_EPS) * g + b).astype(x.dtype)


def apply_rope(x):
    S, dim = x.shape[1], x.shape[-1]
    half = dim // 2
    inv_freq = ROPE_THETA ** (-jnp.arange(half, dtype=jnp.float32) / half)
    ang = jnp.arange(S, dtype=jnp.float32)[:, None] * inv_freq[None, :]
    shape = (S,) + (1,) * (x.ndim - 3) + (half,)
    cos = jnp.cos(ang).reshape(shape)
    sin = jnp.sin(ang).reshape(shape)
    xf = x.astype(jnp.float32)
    x1, x2 = xf[..., :half], xf[..., half:]
    return jnp.concatenate([x1 * cos - x2 * sin, x1 * sin + x2 * cos], axis=-1).astype(x.dtype)


def neighbourhood_attention(q, k, v, rpb):
    B, S, _ = q.shape
    H, dh = NA_HEADS, HEAD_DIM
    R = S // GRID_W
    wr = min(NA_WIN_ROWS, R)
    wc = NA_WIN_COLS
    q = q.reshape(B, R, GRID_W, H, dh)
    k = k.reshape(B, R, GRID_W, H, dh)
    v = v.reshape(B, R, GRID_W, H, dh)
    r_idx = jnp.arange(R)
    rows = jnp.clip(r_idx - wr // 2, 0, R - wr)[:, None] + jnp.arange(wr)[None, :]
    k_rows = k[:, rows]
    v_rows = v[:, rows]
    c_idx = jnp.arange(GRID_W)
    c_start = jnp.clip(c_idx - wc // 2, 0, GRID_W - wc)
    col_in = (c_idx[None, :] >= c_start[:, None]) & (c_idx[None, :] < c_start[:, None] + wc)
    dr = rows - r_idx[:, None] + (NA_WIN_ROWS - 1)
    dc = jnp.clip(c_idx[None, :] - c_idx[:, None] + (NA_WIN_COLS - 1), 0, 2 * NA_WIN_COLS - 2)
    bias = jnp.take(rpb[:, dr], dc, axis=-1)
    bias = bias.transpose(0, 1, 3, 2, 4).astype(jnp.float32)
    s = jnp.einsum('brchd,brjkhd->bhrcjk', q, k_rows).astype(jnp.float32) * (dh ** -0.5) + bias[None]
    s = jnp.where(col_in[:, None, :], s, NEG_INF)
    p = jax.nn.softmax(s.reshape(B, H, R, GRID_W, wr * GRID_W), axis=-1).reshape(s.shape)
    o = jnp.einsum('bhrcjk,brjkhd->brchd', p.astype(v.dtype), v_rows)
    return o.reshape(B, S, H * dh)


def differential_attention(q, k, v, lam_vecs, subln_g, lambda_init):
    B, S, _ = q.shape
    H, dq, dv = DIFF_HEADS, DIFF_QK_DIM, HEAD_DIM
    out_dtype = v.dtype
    qr = apply_rope(q.reshape(B, S, H, 2, dq)).transpose(0, 2, 3, 1, 4)
    kr = apply_rope(k.reshape(B, S, H, 2, dq)).transpose(0, 2, 3, 1, 4)
    vr = v.reshape(B, S, H, dv).transpose(0, 2, 1, 3)
    lf = lam_vecs.astype(jnp.float32)
    lam = jnp.exp(jnp.sum(lf[0] * lf[1])) - jnp.exp(jnp.sum(lf[2] * lf[3])) + lambda_init
    nb = S // DIFF_BLOCK
    q_blocks = qr.reshape(B, H, 2, nb, DIFF_BLOCK, dq).transpose(3, 0, 1, 2, 4, 5)
    scale = dq ** -0.5

    def block(qb):
        s = jnp.einsum('bhiqd,bhikd->bhiqk', qb, kr).astype(jnp.float32) * scale
        p = jax.nn.softmax(s, axis=-1)
        a = p[:, :, 0] - lam * p[:, :, 1]
        return jnp.einsum('bhqk,bhkd->bhqd', a.astype(vr.dtype), vr)

    o = lax.map(block, q_blocks)
    o = o.transpose(1, 0, 3, 2, 4).reshape(B, S, H, dv).astype(jnp.float32)
    o = o * lax.rsqrt(jnp.mean(o * o, axis=-1, keepdims=True) + LN_EPS) * subln_g * (1.0 - lambda_init)
    return o.astype(out_dtype).reshape(B, S, H * dv)


def banded_window_attention(q, k, v, half_width):
    lead = q.shape[:-2]
    L, dh = q.shape[-2], q.shape[-1]
    hw = half_width
    nb = -(-L // hw)
    Lp = nb * hw
    pad = [(0, 0)] * len(lead)
    qp = jnp.pad(q, pad + [(0, Lp - L), (0, 0)]).reshape(*lead, nb, hw, dh)
    kp = jnp.pad(k, pad + [(hw, Lp - L + hw), (0, 0)])
    vp = jnp.pad(v, pad + [(hw, Lp - L + hw), (0, 0)])

    def neighbour_blocks(a):
        return jnp.concatenate([a[..., j * hw:j * hw + Lp, :].reshape(*lead, nb, hw, dh) for j in range(3)], axis=-2)

    kb = neighbour_blocks(kp)
    vb = neighbour_blocks(vp)
    qpos = jnp.arange(Lp).reshape(nb, hw)
    kpos = jnp.arange(nb)[:, None] * hw - hw + jnp.arange(3 * hw)[None, :]
    mask = (jnp.abs(qpos[:, :, None] - kpos[:, None, :]) <= hw) & ((kpos >= 0) & (kpos < L))[:, None, :]
    s = jnp.einsum('...nqd,...nkd->...nqk', qp, kb).astype(jnp.float32) * (dh ** -0.5)
    s = jnp.where(mask, s, NEG_INF)
    m = jnp.max(s, axis=-1)
    p = jnp.exp(s - m[..., None])
    l = jnp.sum(p, axis=-1)
    o = jnp.einsum('...nqk,...nkd->...nqd', p.astype(v.dtype), vb).astype(jnp.float32)
    return (o.reshape(*lead, Lp, dh)[..., :L, :], m.reshape(*lead, Lp)[..., :L], l.reshape(*lead, Lp)[..., :L])


def to_streams(a, dilation):
    B, S, H, dh = a.shape
    return a.reshape(B, S // dilation, dilation, H, dh).transpose(0, 2, 3, 1, 4)


def dilated_attention(q, k, v):
    B, S, _ = q.shape
    H, dh = DIL_HEADS, HEAD_DIM
    out_dtype = v.dtype
    qr = apply_rope(q.reshape(B, S, H, dh))
    kr = apply_rope(k.reshape(B, S, H, dh))
    vr = v.reshape(B, S, H, dh)
    outs, maxes, dens = [], [], []
    for window, dilation in DIL_PATTERNS:
        o, m, l = banded_window_attention(to_streams(qr, dilation), to_streams(kr, dilation),
                                          to_streams(vr, dilation), window // (2 * dilation))
        outs.append(o.transpose(0, 3, 1, 2, 4).reshape(B, S, H, dh))
        maxes.append(m.transpose(0, 3, 1, 2).reshape(B, S, H))
        dens.append(l.transpose(0, 3, 1, 2).reshape(B, S, H))
    m_all = jnp.max(jnp.stack(maxes, axis=0), axis=0)
    wts = [jnp.exp(m - m_all) for m in maxes]
    num = sum(w[..., None] * o for w, o in zip(wts, outs))
    den = sum(w * l for w, l in zip(wts, dens))
    return (num / den[..., None]).astype(out_dtype).reshape(B, S, H * dh)


def encoder_trunk(x, c, w_ada, b_ada, w_in, na_rpb, diff_lambda, diff_subln_g, w_out,
                  ln1_g, ln1_b, w_gu, w_down, ln2_g, ln2_b):
    splits = [NA_WIDTH, 2 * NA_WIDTH, 3 * NA_WIDTH,
              3 * NA_WIDTH + DIFF_WIDTH, 3 * NA_WIDTH + 2 * DIFF_WIDTH, 3 * NA_WIDTH + 3 * DIFF_WIDTH,
              3 * NA_WIDTH + 3 * DIFF_WIDTH + DIL_WIDTH, 3 * NA_WIDTH + 3 * DIFF_WIDTH + 2 * DIL_WIDTH]
    for layer in range(DEPTH):
        lambda_init = 0.8 - 0.6 * math.exp(-0.3 * layer)
        mod = jax.nn.silu(c) @ w_ada[layer] + b_ada[layer]
        sh1, sc1, g1, sh2, sc2, g2 = [m[:, None, :] for m in jnp.split(mod, 6, axis=-1)]
        h = x * (1.0 + sc1) + sh1
        proj = h @ w_in[layer]
        qa, ka, va, qb, kb, vb, qc, kc, vc = jnp.split(proj, splits, axis=-1)
        oa = neighbourhood_attention(qa, ka, va, na_rpb[layer])
        ob = differential_attention(qb, kb, vb, diff_lambda[layer], diff_subln_g[layer], lambda_init)
        oc = dilated_attention(qc, kc, vc)
        mix = jnp.concatenate([oa, ob, oc], axis=-1) @ w_out[layer]
        x = layer_norm(DEEPNORM_ALPHA * x + g1 * mix, ln1_g[layer], ln1_b[layer])
        h = x * (1.0 + sc2) + sh2
        gate, up = jnp.split(h @ w_gu[layer], 2, axis=-1)
        ffn = (jax.nn.silu(gate) * up) @ w_down[layer]
        x = layer_norm(DEEPNORM_ALPHA * x + g2 * ffn, ln2_g[layer], ln2_b[layer])
    return x


def setup_inputs(seed: int = 0) -> dict:
    key = jax.random.key(seed)
    ks = jax.random.split(key, 18)
    f32 = jnp.float32
    nrm = lambda k, shape: jax.random.normal(k, shape, dtype=f32)
    return {
        'x_prompt': nrm(ks[0], (BATCH, SEQ, D_MODEL)),
        'x_sample': nrm(ks[1], (DEC_BATCH, DEC_SEQ, D_MODEL)),
        'c_prompt': nrm(ks[2], (BATCH, D_MODEL)),
        'c_sample': nrm(ks[3], (DEC_BATCH, D_MODEL)),
        'w_ada': nrm(ks[4], (DEPTH, D_MODEL, 6 * D_MODEL)) * D_MODEL ** -0.5,
        'b_ada': 0.01 * nrm(ks[5], (DEPTH, 6 * D_MODEL)),
        'w_in': nrm(ks[6], (DEPTH, D_MODEL, IN_WIDTH)) * D_MODEL ** -0.5,
        'na_rpb': 0.1 * nrm(ks[7], (DEPTH, NA_HEADS, 2 * NA_WIN_ROWS - 1, 2 * NA_WIN_COLS - 1)),
        'diff_lambda': 0.1 * nrm(ks[8], (DEPTH, 4, DIFF_QK_DIM)),
        'diff_subln_g': 1.0 + 0.02 * nrm(ks[9], (DEPTH, HEAD_DIM)),
        'w_out': nrm(ks[10], (DEPTH, MIX_WIDTH, D_MODEL)) * (MIX_WIDTH ** -0.5) * DEEPNORM_BETA,
        'ln1_g': 1.0 + 0.02 * nrm(ks[11], (DEPTH, D_MODEL)),
        'ln1_b': 0.02 * nrm(ks[12], (DEPTH, D_MODEL)),
        'w_gu': nrm(ks[13], (DEPTH, D_MODEL, 2 * FFN_HIDDEN)) * D_MODEL ** -0.5,
        'w_down': nrm(ks[14], (DEPTH, FFN_HIDDEN, D_MODEL)) * (FFN_HIDDEN ** -0.5) * DEEPNORM_BETA,
        'ln2_g': 1.0 + 0.02 * nrm(ks[15], (DEPTH, D_MODEL)),
        'ln2_b': 0.02 * nrm(ks[16], (DEPTH, D_MODEL)),
    }


def reference(x_prompt, x_sample, c_prompt, c_sample, w_ada, b_ada, w_in, na_rpb, diff_lambda,
              diff_subln_g, w_out, ln1_g, ln1_b, w_gu, w_down, ln2_g, ln2_b):
    y_prompt = encoder_trunk(x_prompt, c_prompt, w_ada, b_ada, w_in, na_rpb, diff_lambda, diff_subln_g,
                             w_out, ln1_g, ln1_b, w_gu, w_down, ln2_g, ln2_b)
    y_sample = encoder_trunk(x_sample, c_sample, w_ada, b_ada, w_in, na_rpb, diff_lambda, diff_subln_g,
                             w_out, ln1_g, ln1_b, w_gu, w_down, ln2_g, ln2_b)
    return (y_prompt, y_sample)
```

```python
import functools
import math

import jax
import jax.numpy as jnp
from jax import lax
from jax.experimental import pallas as pl
from jax.experimental.pallas import tpu as pltpu

F32 = jnp.float32
BF16 = jnp.bfloat16

D_MODEL = 1024
DEPTH = 2
HEAD_DIM = 64
NA_HEADS = 4
DIFF_HEADS = 4
DIL_HEADS = 8
NA_WIDTH = NA_HEADS * HEAD_DIM
DIFF_WIDTH = DIFF_HEADS * HEAD_DIM
DIL_WIDTH = DIL_HEADS * HEAD_DIM
MIX_WIDTH = NA_WIDTH + DIFF_WIDTH + DIL_WIDTH
IN_WIDTH = 3 * MIX_WIDTH
AB_WIDTH = 3 * NA_WIDTH + 3 * DIFF_WIDTH
C_WIDTH = 3 * DIL_WIDTH
GRID_W = 64
NA_WIN_ROWS = 8
NA_WIN_COLS = 16
DIFF_QK_DIM = HEAD_DIM // 2
DIL_PATTERNS = ((128, 1), (512, 4), (2048, 16))
DIL_HALF_WIDTH = 64
FFN_HIDDEN = 2816
ROPE_THETA = 10000.0
LN_EPS = 1e-5
DEEPNORM_ALPHA = (2 * DEPTH) ** 0.25
NEG_INF = -1e30

LANES = 128
MXU_WIDTH = 256
VMEM_LIMIT_BYTES = 48 * 1024 * 1024

ROW_TILE = 512
NA_ROWS_PER_STEP = 8
DIFF_TQ = 256
DIFF_TK = 512
DIL_TILE = 1024
FFN_CHUNKS = ((0, 1024), (1024, 2048), (2048, FFN_HIDDEN))

_NT_DIMS = (((1,), (1,)), ((), ()))


def _params(semantics):
    return pltpu.CompilerParams(dimension_semantics=semantics, vmem_limit_bytes=VMEM_LIMIT_BYTES)


def _resident(block_shape, index_map):
    return pl.BlockSpec(block_shape, index_map, pipeline_mode=pl.Buffered(1))


def _ada_kernel(c_ref, w_ref, b_ref, o_ref):
    c = c_ref[...]
    a = c / (1.0 + jnp.exp(-c))
    o_ref[0] = jnp.dot(a, w_ref[0], precision=lax.Precision.HIGHEST,
                       preferred_element_type=F32) + b_ref[0]


def _ada(c_all, w_ada, b_ada):
    nb = c_all.shape[0]
    n_out = w_ada.shape[-1]
    tn = 1536
    return pl.pallas_call(
        _ada_kernel,
        grid=(DEPTH, n_out // tn),
        in_specs=[
            pl.BlockSpec((nb, D_MODEL), lambda l, n: (0, 0)),
            pl.BlockSpec((1, D_MODEL, tn), lambda l, n: (l, 0, n)),
            pl.BlockSpec((1, 1, tn), lambda l, n: (l, 0, n)),
        ],
        out_specs=pl.BlockSpec((1, nb, tn), lambda l, n: (l, 0, n)),
        out_shape=jax.ShapeDtypeStruct((DEPTH, nb, n_out), F32),
        compiler_params=_params(("arbitrary", "arbitrary")),
        name="ada",
    )(c_all, w_ada, b_ada.reshape(DEPTH, 1, n_out))


def _rope_tables(seq, dim):
    half = dim // 2
    inv_freq = ROPE_THETA ** (-jnp.arange(half, dtype=F32) / half)
    ang = jnp.arange(seq, dtype=F32)[:, None] * inv_freq[None, :]
    cos = jnp.cos(ang)
    sin = jnp.sin(ang)
    reps = LANES // dim
    cos_t = jnp.tile(jnp.concatenate([cos, cos], axis=1), (1, reps))
    sin_t = jnp.tile(jnp.concatenate([-sin, sin], axis=1), (1, reps))
    return cos_t, sin_t


def _rope(y, cos, sin_signed, half):
    lane = lax.broadcasted_iota(jnp.int32, y.shape, 1)
    first = (lane & (2 * half - 1)) < half
    partner = jnp.where(first, pltpu.roll(y, LANES - half, axis=1), pltpu.roll(y, half, axis=1))
    return y * cos + partner * sin_signed


_COLUMN_CLASSES = (
    (0, NA_WIDTH, None, HEAD_DIM ** -0.5),
    (NA_WIDTH, 3 * NA_WIDTH, None, None),
    (3 * NA_WIDTH, 3 * NA_WIDTH + DIFF_WIDTH, DIFF_QK_DIM, DIFF_QK_DIM ** -0.5),
    (3 * NA_WIDTH + DIFF_WIDTH, 3 * NA_WIDTH + 2 * DIFF_WIDTH, DIFF_QK_DIM, None),
    (3 * NA_WIDTH + 2 * DIFF_WIDTH, AB_WIDTH, None, None),
    (AB_WIDTH, AB_WIDTH + DIL_WIDTH, HEAD_DIM, HEAD_DIM ** -0.5),
    (AB_WIDTH + DIL_WIDTH, AB_WIDTH + 2 * DIL_WIDTH, HEAD_DIM, None),
    (AB_WIDTH + 2 * DIL_WIDTH, IN_WIDTH, None, None),
)


def _column_class(col):
    for lo, hi, rope_dim, scale in _COLUMN_CLASSES:
        if lo <= col < hi:
            return rope_dim, scale
    raise ValueError(col)


def _inproj_kernel(x_ref, mod_ref, w_ref, cos32_ref, sin32_ref, cos64_ref, sin64_ref, ab_ref, c_ref):
    sh1 = mod_ref[0, 0:1, :]
    sc1 = mod_ref[0, 1:2, :]
    h = (x_ref[0] * (1.0 + sc1) + sh1).astype(BF16)
    for n in range(IN_WIDTH // MXU_WIDTH):
        acc = jnp.dot(h, w_ref[:, n * MXU_WIDTH:(n + 1) * MXU_WIDTH], preferred_element_type=F32)
        for part in range(MXU_WIDTH // LANES):
            col = n * MXU_WIDTH + part * LANES
            y = acc[:, part * LANES:(part + 1) * LANES]
            rope_dim, scale = _column_class(col)
            if rope_dim == DIFF_QK_DIM:
                y = _rope(y, cos32_ref[...], sin32_ref[...], rope_dim // 2)
            elif rope_dim == HEAD_DIM:
                y = _rope(y, cos64_ref[...], sin64_ref[...], rope_dim // 2)
            if scale is not None:
                y = y * scale
            if col < AB_WIDTH:
                ab_ref[0, :, col:col + LANES] = y.astype(ab_ref.dtype)
            else:
                c_ref[0, :, col - AB_WIDTH:col - AB_WIDTH + LANES] = y


def _inproj(x, mod, w_in_bf16, tables):
    b, s, _ = x.shape
    tm = ROW_TILE
    table_spec = pl.BlockSpec((tm, LANES), lambda bi, si: (si, 0))
    return pl.pallas_call(
        _inproj_kernel,
        grid=(b, s // tm),
        in_specs=[
            pl.BlockSpec((1, tm, D_MODEL), lambda bi, si: (bi, si, 0)),
            pl.BlockSpec((1, 6, D_MODEL), lambda bi, si: (bi, 0, 0)),
            _resident((D_MODEL, IN_WIDTH), lambda bi, si: (0, 0)),
            table_spec, table_spec, table_spec, table_spec,
        ],
        out_specs=[
            pl.BlockSpec((1, tm, AB_WIDTH), lambda bi, si: (bi, si, 0)),
            pl.BlockSpec((1, tm, C_WIDTH), lambda bi, si: (bi, si, 0)),
        ],
        out_shape=[
            jax.ShapeDtypeStruct((b, s, AB_WIDTH), BF16),
            jax.ShapeDtypeStruct((b, s, C_WIDTH), F32),
        ],
        compiler_params=_params(("arbitrary", "arbitrary")),
        name="inproj",
    )(x, mod, w_in_bf16, *tables)


def _na_bias_table(rpb):
    c_idx = jnp.arange(GRID_W)
    c_start = jnp.clip(c_idx - NA_WIN_COLS // 2, 0, GRID_W - NA_WIN_COLS)
    col_in = (c_idx[None, :] >= c_start[:, None]) & (c_idx[None, :] < c_start[:, None] + NA_WIN_COLS)
    dc = jnp.clip(c_idx[None, :] - c_idx[:, None] + (NA_WIN_COLS - 1), 0, 2 * NA_WIN_COLS - 2)
    v_idx = jnp.arange(NA_WIN_ROWS)
    j_idx = jnp.arange(NA_WIN_ROWS)
    dr = j_idx[None, :] - v_idx[:, None] + (NA_WIN_ROWS - 1)
    bias = rpb.astype(F32)[:, dr][:, :, :, dc]
    bias = jnp.where(col_in[None, None, None], bias, NEG_INF)
    bias = bias.transpose(0, 1, 3, 2, 4)
    return bias.reshape(NA_HEADS, NA_WIN_ROWS, GRID_W, NA_WIN_ROWS * GRID_W)


def _na_kernel(q_ref, k_ref, v_ref, bias_ref, o_ref, *, n_rows):
    step = pl.program_id(2)
    lane = lax.broadcasted_iota(jnp.int32, (GRID_W, LANES), 1)
    first = lane < HEAD_DIM
    win = NA_WIN_ROWS * GRID_W
    for rr in range(NA_ROWS_PER_STEP):
        r = step * NA_ROWS_PER_STEP + rr
        row0 = jnp.clip(r - NA_WIN_ROWS // 2, 0, n_rows - NA_WIN_ROWS)
        variant = r - row0
        start = pl.multiple_of(row0 * GRID_W, GRID_W)
        kw = k_ref[0, pl.ds(start, win), :]
        vw = v_ref[0, pl.ds(start, win), :]
        q = q_ref[0, rr * GRID_W:(rr + 1) * GRID_W, :]
        outs = []
        for h in range(2):
            qm = jnp.where(first if h == 0 else jnp.logical_not(first), q, jnp.zeros_like(q))
            s = lax.dot_general(qm, kw, _NT_DIMS, preferred_element_type=F32) + bias_ref[h, variant]
            m = jnp.max(s, axis=1, keepdims=True)
            p = jnp.exp(s - m)
            l = jnp.sum(p, axis=1, keepdims=True)
            o = jnp.dot(p.astype(BF16), vw, preferred_element_type=F32)
            outs.append(o / l)
        o_ref[0, rr * GRID_W:(rr + 1) * GRID_W, :] = jnp.where(first, outs[0], outs[1]).astype(o_ref.dtype)


def _na(ab, bias_tbl):
    b, s, _ = ab.shape
    n_rows = s // GRID_W
    tq = NA_ROWS_PER_STEP * GRID_W
    k_blk = NA_WIDTH // LANES
    return pl.pallas_call(
        functools.partial(_na_kernel, n_rows=n_rows),
        grid=(b, NA_HEADS // 2, s // tq),
        in_specs=[
            pl.BlockSpec((1, tq, LANES), lambda bi, hp, i: (bi, i, hp)),
            pl.BlockSpec((1, s, LANES), lambda bi, hp, i: (bi, 0, k_blk + hp)),
            pl.BlockSpec((1, s, LANES), lambda bi, hp, i: (bi, 0, 2 * k_blk + hp)),
            pl.BlockSpec((2, NA_WIN_ROWS, GRID_W, NA_WIN_ROWS * GRID_W), lambda bi, hp, i: (hp, 0, 0, 0)),
        ],
        out_specs=pl.BlockSpec((1, tq, LANES), lambda bi, hp, i: (bi, i, hp)),
        out_shape=jax.ShapeDtypeStruct((b, s, NA_WIDTH), BF16),
        compiler_params=_params(("arbitrary", "arbitrary", "arbitrary")),
        name="na",
    )(ab, ab, ab, bias_tbl)


def _diff_kernel(q_ref, k_ref, v_ref, lam_ref, g_ref, o_ref, m_scr, l_scr, acc_scr, *, seq, lambda_init):
    q = q_ref[0]
    lane = lax.broadcasted_iota(jnp.int32, (DIFF_TQ, LANES), 1)
    group = lax.shift_right_logical(lane, DIFF_QK_DIM.bit_length() - 1)
    qms = [jnp.where(group == c, q, jnp.zeros_like(q)) for c in range(4)]
    m_scr[...] = jnp.full(m_scr.shape, NEG_INF, F32)
    l_scr[...] = jnp.zeros(l_scr.shape, F32)
    acc_scr[...] = jnp.zeros(acc_scr.shape, F32)

    def body(j, carry):
        start = pl.multiple_of(j * DIFF_TK, DIFF_TK)
        kt = k_ref[0, pl.ds(start, DIFF_TK), :]
        vt = v_ref[0, pl.ds(start, DIFF_TK), :]
        for c in range(4):
            s = lax.dot_general(qms[c], kt, _NT_DIMS, preferred_element_type=F32)
            m_old = m_scr[c]
            m_new = jnp.maximum(m_old, jnp.max(s, axis=1, keepdims=True))
            p = jnp.exp(s - m_new)
            alpha = jnp.exp(m_old - m_new)
            l_scr[c] = alpha * l_scr[c] + jnp.sum(p, axis=1, keepdims=True)
            acc_scr[c] = alpha * acc_scr[c] + jnp.dot(p.astype(BF16), vt, preferred_element_type=F32)
            m_scr[c] = m_new
        return carry

    lax.fori_loop(0, seq // DIFF_TK, body, 0)

    lf = lam_ref[...]
    lam = (jnp.exp(jnp.sum(lf[0:1] * lf[1:2], axis=1, keepdims=True))
           - jnp.exp(jnp.sum(lf[2:3] * lf[3:4], axis=1, keepdims=True)) + lambda_init)
    o = [acc_scr[c] / l_scr[c] for c in range(4)]
    first = lane < HEAD_DIM
    d = jnp.where(first, o[0] - lam * o[1], o[2] - lam * o[3])
    sq = d * d
    ms_a = jnp.sum(jnp.where(first, sq, 0.0), axis=1, keepdims=True) * (1.0 / HEAD_DIM)
    ms_b = jnp.sum(jnp.where(first, 0.0, sq), axis=1, keepdims=True) * (1.0 / HEAD_DIM)
    ms = jnp.where(first, ms_a, ms_b)
    out = d * lax.rsqrt(ms + LN_EPS) * g_ref[...] * (1.0 - lambda_init)
    o_ref[0] = out.astype(o_ref.dtype)


def _diff(ab, lam_vecs, subln_g, lambda_init):
    b, s, _ = ab.shape
    q_blk = 3 * NA_WIDTH // LANES
    k_blk = q_blk + DIFF_WIDTH // LANES
    v_blk = k_blk + DIFF_WIDTH // LANES
    g_tile = jnp.tile(subln_g.astype(F32), LANES // HEAD_DIM).reshape(1, LANES)
    return pl.pallas_call(
        functools.partial(_diff_kernel, seq=s, lambda_init=lambda_init),
        grid=(b, DIFF_HEADS // 2, s // DIFF_TQ),
        in_specs=[
            pl.BlockSpec((1, DIFF_TQ, LANES), lambda bi, hp, i: (bi, i, q_blk + hp)),
            pl.BlockSpec((1, s, LANES), lambda bi, hp, i: (bi, 0, k_blk + hp)),
            pl.BlockSpec((1, s, LANES), lambda bi, hp, i: (bi, 0, v_blk + hp)),
            pl.BlockSpec((4, DIFF_QK_DIM), lambda bi, hp, i: (0, 0)),
            pl.BlockSpec((1, LANES), lambda bi, hp, i: (0, 0)),
        ],
        out_specs=pl.BlockSpec((1, DIFF_TQ, LANES), lambda bi, hp, i: (bi, i, hp)),
        out_shape=jax.ShapeDtypeStruct((b, s, DIFF_WIDTH), BF16),
        scratch_shapes=[
            pltpu.VMEM((4, DIFF_TQ, 1), F32),
            pltpu.VMEM((4, DIFF_TQ, 1), F32),
            pltpu.VMEM((4, DIFF_TQ, LANES), F32),
        ],
        compiler_params=_params(("arbitrary", "arbitrary", "arbitrary")),
        name="diff",
    )(ab, ab, ab, lam_vecs.astype(F32), g_tile)


def _dil_kernel(q_ref, k_ref, v_ref, o_ref, o_scr, m_scr, l_scr, *, seq):
    tile = pl.program_id(2)
    hw = DIL_HALF_WIDTH
    lane = lax.broadcasted_iota(jnp.int32, (hw, LANES), 1)
    first = lane < HEAD_DIM
    row = lax.broadcasted_iota(jnp.int32, (hw, 3 * hw), 0)
    col = lax.broadcasted_iota(jnp.int32, (hw, 3 * hw), 1)
    delta = col - hw - row
    band = jnp.minimum(delta + hw, hw - delta)

    for pat, (_, dil) in enumerate(DIL_PATTERNS):
        shift = dil.bit_length() - 1
        blocks_per_tile = DIL_TILE // (hw * dil)
        n_blocks = seq // (hw * dil)

        def unit(u, carry, pat=pat, dil=dil, shift=shift, blocks_per_tile=blocks_per_tile, n_blocks=n_blocks):
            res = u & (dil - 1)
            lb = lax.shift_right_logical(u, shift)
            blk = tile * blocks_per_tile + lb
            q_start = lb * (hw * dil) + res
            q = q_ref[0, pl.ds(q_start, hw, stride=dil), :].astype(BF16)
            ks, vs = [], []
            for j in (-1, 0, 1):
                kb = jnp.clip(blk + j, 0, n_blocks - 1)
                k_start = kb * (hw * dil) + res
                ks.append(k_ref[0, pl.ds(k_start, hw, stride=dil), :])
                vs.append(v_ref[0, pl.ds(k_start, hw, stride=dil), :])
            kcat = jnp.concatenate(ks, axis=0).astype(BF16)
            vcat = jnp.concatenate(vs, axis=0).astype(BF16)
            col_min = jnp.where(blk > 0, 0, hw)
            col_max = jnp.where(blk < n_blocks - 1, 3 * hw - 1, 2 * hw - 1)
            valid = jnp.minimum(band, jnp.minimum(col - col_min, col_max - col)) >= 0
            outs, ms, ls = [], [], []
            for h in range(2):
                qm = jnp.where(first if h == 0 else jnp.logical_not(first), q, jnp.zeros_like(q))
                s = lax.dot_general(qm, kcat, _NT_DIMS, preferred_element_type=F32)
                s = jnp.where(valid, s, NEG_INF)
                m = jnp.max(s, axis=1, keepdims=True)
                p = jnp.exp(s - m)
                ls.append(jnp.sum(p, axis=1, keepdims=True))
                ms.append(m)
                outs.append(jnp.dot(p.astype(BF16), vcat, preferred_element_type=F32))
            rows = pl.ds(q_start, hw, stride=dil)
            o_scr[pat, rows, :] = jnp.where(first, outs[0], outs[1])
            m_scr[pat, rows, :] = jnp.where(first, ms[0], ms[1])
            l_scr[pat, rows, :] = jnp.where(first, ls[0], ls[1])
            return carry

        lax.fori_loop(0, DIL_TILE // hw, unit, 0)

    m_all = jnp.maximum(jnp.maximum(m_scr[0], m_scr[1]), m_scr[2])
    num = jnp.zeros((DIL_TILE, LANES), F32)
    den = jnp.zeros((DIL_TILE, LANES), F32)
    for pat in range(len(DIL_PATTERNS)):
        w = jnp.exp(m_scr[pat] - m_all)
        num = num + w * o_scr[pat]
        den = den + w * l_scr[pat]
    o_ref[0] = (num / den).astype(o_ref.dtype)


def _dil(c):
    b, s, _ = c.shape
    k_blk = DIL_WIDTH // LANES
    n_pat = len(DIL_PATTERNS)
    return pl.pallas_call(
        functools.partial(_dil_kernel, seq=s),
        grid=(b, DIL_HEADS // 2, s // DIL_TILE),
        in_specs=[
            pl.BlockSpec((1, DIL_TILE, LANES), lambda bi, hp, i: (bi, i, hp)),
            pl.BlockSpec((1, s, LANES), lambda bi, hp, i: (bi, 0, k_blk + hp)),
            pl.BlockSpec((1, s, LANES), lambda bi, hp, i: (bi, 0, 2 * k_blk + hp)),
        ],
        out_specs=pl.BlockSpec((1, DIL_TILE, LANES), lambda bi, hp, i: (bi, i, hp)),
        out_shape=jax.ShapeDtypeStruct((b, s, DIL_WIDTH), BF16),
        scratch_shapes=[
            pltpu.VMEM((n_pat, DIL_TILE, LANES), F32),
            pltpu.VMEM((n_pat, DIL_TILE, LANES), F32),
            pltpu.VMEM((n_pat, DIL_TILE, LANES), F32),
        ],
        compiler_params=_params(("arbitrary", "arbitrary", "arbitrary")),
        name="dil",
    )(c, c, c)


def _layer_norm(z, g, b):
    mu = jnp.mean(z, axis=1, keepdims=True)
    zc = z - mu
    var = jnp.mean(zc * zc, axis=1, keepdims=True)
    return zc * lax.rsqrt(var + LN_EPS) * g + b


def _outproj_kernel(oa_ref, ob_ref, oc_ref, x_ref, mod_ref, w_ref, g_ref, b_ref, y_ref):
    mix = jnp.dot(oa_ref[0], w_ref[0:NA_WIDTH, :], preferred_element_type=F32)
    mix = mix + jnp.dot(ob_ref[0], w_ref[NA_WIDTH:NA_WIDTH + DIFF_WIDTH, :], preferred_element_type=F32)
    mix = mix + jnp.dot(oc_ref[0], w_ref[NA_WIDTH + DIFF_WIDTH:MIX_WIDTH, :], preferred_element_type=F32)
    g1 = mod_ref[0, 2:3, :]
    z = DEEPNORM_ALPHA * x_ref[0] + g1 * mix
    y_ref[0] = _layer_norm(z, g_ref[...], b_ref[...])


def _outproj(oa, ob, oc, x, mod, w_out_bf16, ln_g, ln_b):
    b, s, _ = x.shape
    tm = ROW_TILE
    row = lambda width: pl.BlockSpec((1, tm, width), lambda bi, si: (bi, si, 0))
    vec = pl.BlockSpec((1, D_MODEL), lambda bi, si: (0, 0))
    return pl.pallas_call(
        _outproj_kernel,
        grid=(b, s // tm),
        in_specs=[
            row(NA_WIDTH), row(DIFF_WIDTH), row(DIL_WIDTH), row(D_MODEL),
            pl.BlockSpec((1, 6, D_MODEL), lambda bi, si: (bi, 0, 0)),
            _resident((MIX_WIDTH, D_MODEL), lambda bi, si: (0, 0)),
            vec, vec,
        ],
        out_specs=row(D_MODEL),
        out_shape=jax.ShapeDtypeStruct((b, s, D_MODEL), F32),
        compiler_params=_params(("arbitrary", "arbitrary")),
        name="outproj",
    )(oa, ob, oc, x, mod, w_out_bf16, ln_g.reshape(1, D_MODEL), ln_b.reshape(1, D_MODEL))


def _ffn_kernel(x_ref, mod_ref, wgu_ref, wd_ref, g_ref, b_ref, y_ref):
    x = x_ref[0]
    sh2 = mod_ref[0, 3:4, :]
    sc2 = mod_ref[0, 4:5, :]
    g2 = mod_ref[0, 5:6, :]
    h = (x * (1.0 + sc2) + sh2).astype(BF16)
    acc = jnp.zeros((x.shape[0], D_MODEL), F32)
    for c0, c1 in FFN_CHUNKS:
        gate = jnp.dot(h, wgu_ref[:, c0:c1], preferred_element_type=F32)
        up = jnp.dot(h, wgu_ref[:, FFN_HIDDEN + c0:FFN_HIDDEN + c1], preferred_element_type=F32)
        act = (gate / (1.0 + jnp.exp(-gate)) * up).astype(BF16)
        acc = acc + jnp.dot(act, wd_ref[c0:c1, :], preferred_element_type=F32)
    z = DEEPNORM_ALPHA * x + g2 * acc
    y_ref[0] = _layer_norm(z, g_ref[...], b_ref[...])


def _ffn(x, mod, w_gu_bf16, w_down_bf16, ln_g, ln_b):
    b, s, _ = x.shape
    tm = ROW_TILE
    row = pl.BlockSpec((1, tm, D_MODEL), lambda bi, si: (bi, si, 0))
    vec = pl.BlockSpec((1, D_MODEL), lambda bi, si: (0, 0))
    return pl.pallas_call(
        _ffn_kernel,
        grid=(b, s // tm),
        in_specs=[
            row,
            pl.BlockSpec((1, 6, D_MODEL), lambda bi, si: (bi, 0, 0)),
            _resident((D_MODEL, 2 * FFN_HIDDEN), lambda bi, si: (0, 0)),
            _resident((FFN_HIDDEN, D_MODEL), lambda bi, si: (0, 0)),
            vec, vec,
        ],
        out_specs=row,
        out_shape=jax.ShapeDtypeStruct((b, s, D_MODEL), F32),
        compiler_params=_params(("arbitrary", "arbitrary")),
        name="ffn",
    )(x, mod, w_gu_bf16, w_down_bf16, ln_g.reshape(1, D_MODEL), ln_b.reshape(1, D_MODEL))


def _layer(x, mod, layer, tables, w_in_b, bias_tbl, diff_lambda, diff_subln_g, w_out_b,
           ln1_g, ln1_b, w_gu_b, w_down_b, ln2_g, ln2_b):
    lambda_init = 0.8 - 0.6 * math.exp(-0.3 * layer)
    ab, c = _inproj(x, mod, w_in_b, tables)
    oa = _na(ab, bias_tbl)
    ob = _diff(ab, diff_lambda, diff_subln_g, lambda_init)
    oc = _dil(c)
    x = _outproj(oa, ob, oc, x, mod, w_out_b, ln1_g, ln1_b)
    return _ffn(x, mod, w_gu_b, w_down_b, ln2_g, ln2_b)


def kernel(x_prompt, x_sample, c_prompt, c_sample, w_ada, b_ada, w_in, na_rpb, diff_lambda, diff_subln_g,
           w_out, ln1_g, ln1_b, w_gu, w_down, ln2_g, ln2_b):
    xs = [x_prompt, x_sample]
    n_prompt = c_prompt.shape[0]
    mod_all = _ada(jnp.concatenate([c_prompt, c_sample], axis=0), w_ada, b_ada)
    tables = [_rope_tables(x.shape[1], DIFF_QK_DIM) + _rope_tables(x.shape[1], HEAD_DIM) for x in xs]
    for layer in range(DEPTH):
        w_in_b = w_in[layer].astype(BF16)
        w_out_b = w_out[layer].astype(BF16)
        w_gu_b = w_gu[layer].astype(BF16)
        w_down_b = w_down[layer].astype(BF16)
        bias_tbl = _na_bias_table(na_rpb[layer])
        for g in range(2):
            nb = xs[g].shape[0]
            lo = 0 if g == 0 else n_prompt
            mod = mod_all[layer, lo:lo + nb].reshape(nb, 6, D_MODEL)
            xs[g] = _layer(xs[g], mod, layer, tables[g], w_in_b, bias_tbl, diff_lambda[layer],
                           diff_subln_g[layer], w_out_b, ln1_g[layer], ln1_b[layer], w_gu_b, w_down_b,
                           ln2_g[layer], ln2_b[layer])
    return tuple(xs)
```

```python
import functools
import math

import jax
import jax.numpy as jnp
from jax import lax
from jax.experimental import pallas as pl
from jax.experimental.pallas import tpu as pltpu

F32 = jnp.float32
BF16 = jnp.bfloat16

D_MODEL = 1024
DEPTH = 2
HEAD_DIM = 64
NA_HEADS = 4
DIFF_HEADS = 4
DIL_HEADS = 8
NA_WIDTH = NA_HEADS * HEAD_DIM
DIFF_WIDTH = DIFF_HEADS * HEAD_DIM
DIL_WIDTH = DIL_HEADS * HEAD_DIM
MIX_WIDTH = NA_WIDTH + DIFF_WIDTH + DIL_WIDTH
IN_WIDTH = 3 * MIX_WIDTH
AB_WIDTH = 3 * NA_WIDTH + 3 * DIFF_WIDTH
C_WIDTH = 3 * DIL_WIDTH
GRID_W = 64
NA_WIN_ROWS = 8
NA_WIN_COLS = 16
DIFF_QK_DIM = HEAD_DIM // 2
DIL_PATTERNS = ((128, 1), (512, 4), (2048, 16))
DIL_HALF_WIDTH = 64
FFN_HIDDEN = 2816
ROPE_THETA = 10000.0
LN_EPS = 1e-5
DEEPNORM_ALPHA = (2 * DEPTH) ** 0.25
NEG_INF = -1e30
LOG2E = math.log2(math.e)

LANES = 128
MXU_WIDTH = 256
VMEM_LIMIT_BYTES = 48 * 1024 * 1024

ROW_TILE = 512
NA_ROWS_PER_STEP = 8
DIFF_TQ = 256
DIFF_TK = 512
DIL_TILE = 1024
DIL_GROUP = 8
FFN_CHUNKS = ((0, 1024), (1024, 2048), (2048, FFN_HIDDEN))

_NT_DIMS = (((1,), (1,)), ((), ()))


def _params(semantics):
    return pltpu.CompilerParams(dimension_semantics=semantics, vmem_limit_bytes=VMEM_LIMIT_BYTES)


def _resident(block_shape, index_map):
    return pl.BlockSpec(block_shape, index_map, pipeline_mode=pl.Buffered(1))


def _ada_kernel(c_ref, w_ref, b_ref, o_ref):
    c = c_ref[...]
    a = c / (1.0 + jnp.exp(-c))
    o_ref[0] = jnp.dot(a, w_ref[0], precision=lax.Precision.HIGHEST,
                       preferred_element_type=F32) + b_ref[0]


def _ada(c_all, w_ada, b_ada):
    nb = c_all.shape[0]
    n_out = w_ada.shape[-1]
    tn = 1536
    return pl.pallas_call(
        _ada_kernel,
        grid=(DEPTH, n_out // tn),
        in_specs=[
            pl.BlockSpec((nb, D_MODEL), lambda l, n: (0, 0)),
            pl.BlockSpec((1, D_MODEL, tn), lambda l, n: (l, 0, n)),
            pl.BlockSpec((1, 1, tn), lambda l, n: (l, 0, n)),
        ],
        out_specs=pl.BlockSpec((1, nb, tn), lambda l, n: (l, 0, n)),
        out_shape=jax.ShapeDtypeStruct((DEPTH, nb, n_out), F32),
        compiler_params=_params(("arbitrary", "arbitrary")),
        name="ada",
    )(c_all, w_ada, b_ada.reshape(DEPTH, 1, n_out))


def _rope_tables(seq, dim):
    half = dim // 2
    inv_freq = ROPE_THETA ** (-jnp.arange(half, dtype=F32) / half)
    ang = jnp.arange(seq, dtype=F32)[:, None] * inv_freq[None, :]
    cos = jnp.cos(ang)
    sin = jnp.sin(ang)
    reps = LANES // dim
    cos_t = jnp.tile(jnp.concatenate([cos, cos], axis=1), (1, reps))
    sin_t = jnp.tile(jnp.concatenate([-sin, sin], axis=1), (1, reps))
    return cos_t, sin_t


def _rope(y, cos, sin_signed, half):
    lane = lax.broadcasted_iota(jnp.int32, y.shape, 1)
    first = (lane & (2 * half - 1)) < half
    partner = jnp.where(first, pltpu.roll(y, LANES - half, axis=1), pltpu.roll(y, half, axis=1))
    return y * cos + partner * sin_signed


_COLUMN_CLASSES = (
    (0, NA_WIDTH, None, HEAD_DIM ** -0.5),
    (NA_WIDTH, 3 * NA_WIDTH, None, None),
    (3 * NA_WIDTH, 3 * NA_WIDTH + DIFF_WIDTH, DIFF_QK_DIM, DIFF_QK_DIM ** -0.5 * LOG2E),
    (3 * NA_WIDTH + DIFF_WIDTH, 3 * NA_WIDTH + 2 * DIFF_WIDTH, DIFF_QK_DIM, None),
    (3 * NA_WIDTH + 2 * DIFF_WIDTH, AB_WIDTH, None, None),
    (AB_WIDTH, AB_WIDTH + DIL_WIDTH, HEAD_DIM, HEAD_DIM ** -0.5),
    (AB_WIDTH + DIL_WIDTH, AB_WIDTH + 2 * DIL_WIDTH, HEAD_DIM, None),
    (AB_WIDTH + 2 * DIL_WIDTH, IN_WIDTH, None, None),
)


def _column_class(col):
    for lo, hi, rope_dim, scale in _COLUMN_CLASSES:
        if lo <= col < hi:
            return rope_dim, scale
    raise ValueError(col)


def _inproj_kernel(x_ref, mod_ref, w_ref, cos32_ref, sin32_ref, cos64_ref, sin64_ref, ab_ref, c_ref):
    sh1 = mod_ref[0, 0:1, :]
    sc1 = mod_ref[0, 1:2, :]
    h = (x_ref[0] * (1.0 + sc1) + sh1).astype(BF16)
    for n in range(IN_WIDTH // MXU_WIDTH):
        acc = jnp.dot(h, w_ref[:, n * MXU_WIDTH:(n + 1) * MXU_WIDTH], preferred_element_type=F32)
        for part in range(MXU_WIDTH // LANES):
            col = n * MXU_WIDTH + part * LANES
            y = acc[:, part * LANES:(part + 1) * LANES]
            rope_dim, scale = _column_class(col)
            if rope_dim == DIFF_QK_DIM:
                y = _rope(y, cos32_ref[...], sin32_ref[...], rope_dim // 2)
            elif rope_dim == HEAD_DIM:
                y = _rope(y, cos64_ref[...], sin64_ref[...], rope_dim // 2)
            if scale is not None:
                y = y * scale
            if col < AB_WIDTH:
                ab_ref[0, :, col:col + LANES] = y.astype(ab_ref.dtype)
            else:
                c_ref[0, :, col - AB_WIDTH:col - AB_WIDTH + LANES] = y


def _inproj(x, mod, w_in_bf16, tables):
    b, s, _ = x.shape
    tm = ROW_TILE
    table_spec = pl.BlockSpec((tm, LANES), lambda bi, si: (si, 0))
    return pl.pallas_call(
        _inproj_kernel,
        grid=(b, s // tm),
        in_specs=[
            pl.BlockSpec((1, tm, D_MODEL), lambda bi, si: (bi, si, 0)),
            pl.BlockSpec((1, 6, D_MODEL), lambda bi, si: (bi, 0, 0)),
            _resident((D_MODEL, IN_WIDTH), lambda bi, si: (0, 0)),
            table_spec, table_spec, table_spec, table_spec,
        ],
        out_specs=[
            pl.BlockSpec((1, tm, AB_WIDTH), lambda bi, si: (bi, si, 0)),
            pl.BlockSpec((1, tm, C_WIDTH), lambda bi, si: (bi, si, 0)),
        ],
        out_shape=[
            jax.ShapeDtypeStruct((b, s, AB_WIDTH), BF16),
            jax.ShapeDtypeStruct((b, s, C_WIDTH), F32),
        ],
        compiler_params=_params(("arbitrary", "arbitrary")),
        name="inproj",
    )(x, mod, w_in_bf16, *tables)


def _na_bias_table(rpb):
    c_idx = jnp.arange(GRID_W)
    c_start = jnp.clip(c_idx - NA_WIN_COLS // 2, 0, GRID_W - NA_WIN_COLS)
    col_in = (c_idx[None, :] >= c_start[:, None]) & (c_idx[None, :] < c_start[:, None] + NA_WIN_COLS)
    dc = jnp.clip(c_idx[None, :] - c_idx[:, None] + (NA_WIN_COLS - 1), 0, 2 * NA_WIN_COLS - 2)
    v_idx = jnp.arange(NA_WIN_ROWS)
    j_idx = jnp.arange(NA_WIN_ROWS)
    dr = j_idx[None, :] - v_idx[:, None] + (NA_WIN_ROWS - 1)
    bias = rpb.astype(F32)[:, dr][:, :, :, dc]
    bias = jnp.where(col_in[None, None, None], bias, NEG_INF)
    bias = bias.transpose(0, 1, 3, 2, 4)
    bias = bias.reshape(NA_HEADS // 2, 2, NA_WIN_ROWS, GRID_W, NA_WIN_ROWS * GRID_W)
    return bias.transpose(0, 2, 1, 3, 4).reshape(NA_HEADS // 2, NA_WIN_ROWS, 2 * GRID_W, NA_WIN_ROWS * GRID_W)


def _na_kernel(q_ref, k_ref, v_ref, bias_ref, o_ref, *, n_rows):
    step = pl.program_id(2)
    lane = lax.broadcasted_iota(jnp.int32, (GRID_W, LANES), 1)
    first = lane < HEAD_DIM
    win = NA_WIN_ROWS * GRID_W
    scores, vws = [], []
    for rr in range(NA_ROWS_PER_STEP):
        r = step * NA_ROWS_PER_STEP + rr
        row0 = jnp.clip(r - NA_WIN_ROWS // 2, 0, n_rows - NA_WIN_ROWS)
        start = pl.multiple_of(row0 * GRID_W, GRID_W)
        kw = k_ref[0, pl.ds(start, win), :]
        vws.append(v_ref[0, pl.ds(start, win), :])
        q = q_ref[0, rr * GRID_W:(rr + 1) * GRID_W, :]
        zero = jnp.zeros_like(q)
        q2 = jnp.concatenate([jnp.where(first, q, zero), jnp.where(first, zero, q)], axis=0)
        scores.append(lax.dot_general(q2, kw, _NT_DIMS, preferred_element_type=F32) + bias_ref[0, r - row0])
    probs, dens = [], []
    for s in scores:
        p = jnp.exp(s - jnp.max(s, axis=1, keepdims=True))
        dens.append(jnp.sum(p, axis=1, keepdims=True))
        probs.append(p.astype(BF16))
    for rr, (p, l, vw) in enumerate(zip(probs, dens, vws)):
        o2 = jnp.dot(p, vw, preferred_element_type=F32) / l
        o_ref[0, rr * GRID_W:(rr + 1) * GRID_W, :] = jnp.where(first, o2[:GRID_W], o2[GRID_W:]).astype(o_ref.dtype)


def _na(ab, bias_tbl):
    b, s, _ = ab.shape
    n_rows = s // GRID_W
    tq = NA_ROWS_PER_STEP * GRID_W
    k_blk = NA_WIDTH // LANES
    return pl.pallas_call(
        functools.partial(_na_kernel, n_rows=n_rows),
        grid=(b, NA_HEADS // 2, s // tq),
        in_specs=[
            pl.BlockSpec((1, tq, LANES), lambda bi, hp, i: (bi, i, hp)),
            pl.BlockSpec((1, s, LANES), lambda bi, hp, i: (bi, 0, k_blk + hp)),
            pl.BlockSpec((1, s, LANES), lambda bi, hp, i: (bi, 0, 2 * k_blk + hp)),
            pl.BlockSpec((1, NA_WIN_ROWS, 2 * GRID_W, NA_WIN_ROWS * GRID_W), lambda bi, hp, i: (hp, 0, 0, 0)),
        ],
        out_specs=pl.BlockSpec((1, tq, LANES), lambda bi, hp, i: (bi, i, hp)),
        out_shape=jax.ShapeDtypeStruct((b, s, NA_WIDTH), BF16),
        compiler_params=_params(("arbitrary", "arbitrary", "arbitrary")),
        name="na",
    )(ab, ab, ab, bias_tbl)


def _diff_kernel(q_ref, k_ref, v_ref, lam_ref, g_ref, o_ref, vt_scr, acc_scr, *, seq, lambda_init):
    n_kv = seq // DIFF_TK

    @pl.when(pl.program_id(2) == 0)
    def _():
        for t in range(n_kv):
            blk = v_ref[0, t * DIFF_TK:(t + 1) * DIFF_TK, :].astype(F32)
            vt_scr[t] = blk.T.astype(BF16)

    q = q_ref[0]
    lane = lax.broadcasted_iota(jnp.int32, (DIFF_TQ, LANES), 1)
    group = lax.shift_right_logical(lane, DIFF_QK_DIM.bit_length() - 1)
    qms = [jnp.where(group == c, q, jnp.zeros_like(q)) for c in range(4)]
    acc_scr[...] = jnp.zeros(acc_scr.shape, F32)

    def body(j, carry):
        ms, ls = carry
        start = pl.multiple_of(j * DIFF_TK, DIFF_TK)
        kt = k_ref[0, pl.ds(start, DIFF_TK), :]
        vt = vt_scr[j]
        sts = [lax.dot_general(kt, qms[c], _NT_DIMS, preferred_element_type=F32) for c in range(4)]
        new_ms, new_ls, alphas, probs = [], [], [], []
        for c in range(4):
            m_new = jnp.maximum(ms[c], jnp.max(sts[c], axis=0, keepdims=True))
            p = jnp.exp2(sts[c] - m_new)
            alpha = jnp.exp2(ms[c] - m_new)
            new_ls.append(alpha * ls[c] + jnp.sum(p, axis=0, keepdims=True))
            new_ms.append(m_new)
            alphas.append(alpha)
            probs.append(p.astype(BF16))
        for c in range(4):
            acc_scr[c] = alphas[c] * acc_scr[c] + jnp.dot(vt, probs[c], preferred_element_type=F32)
        return tuple(new_ms), tuple(new_ls)

    init = (tuple(jnp.full((1, DIFF_TQ), NEG_INF, F32) for _ in range(4)),
            tuple(jnp.zeros((1, DIFF_TQ), F32) for _ in range(4)))
    _, ls = lax.fori_loop(0, n_kv, body, init)

    lf = lam_ref[...]
    lam = (jnp.exp(jnp.sum(lf[0:1] * lf[1:2], axis=1, keepdims=True))
           - jnp.exp(jnp.sum(lf[2:3] * lf[3:4], axis=1, keepdims=True)) + lambda_init)
    o = [acc_scr[c] / ls[c] for c in range(4)]
    feat = lax.broadcasted_iota(jnp.int32, (LANES, DIFF_TQ), 0)
    first = feat < HEAD_DIM
    d = jnp.where(first, o[0] - lam * o[1], o[2] - lam * o[3])
    sq = d * d
    ms_a = jnp.sum(jnp.where(first, sq, 0.0), axis=0, keepdims=True) * (1.0 / HEAD_DIM)
    ms_b = jnp.sum(jnp.where(first, 0.0, sq), axis=0, keepdims=True) * (1.0 / HEAD_DIM)
    normed = d * lax.rsqrt(jnp.where(first, ms_a, ms_b) + LN_EPS)
    o_ref[0] = (normed.T * g_ref[...] * (1.0 - lambda_init)).astype(o_ref.dtype)


def _diff(ab, lam_vecs, subln_g, lambda_init):
    b, s, _ = ab.shape
    q_blk = 3 * NA_WIDTH // LANES
    k_blk = q_blk + DIFF_WIDTH // LANES
    v_blk = k_blk + DIFF_WIDTH // LANES
    g_tile = jnp.tile(subln_g.astype(F32), LANES // HEAD_DIM).reshape(1, LANES)
    return pl.pallas_call(
        functools.partial(_diff_kernel, seq=s, lambda_init=lambda_init),
        grid=(b, DIFF_HEADS // 2, s // DIFF_TQ),
        in_specs=[
            pl.BlockSpec((1, DIFF_TQ, LANES), lambda bi, hp, i: (bi, i, q_blk + hp)),
            pl.BlockSpec((1, s, LANES), lambda bi, hp, i: (bi, 0, k_blk + hp)),
            pl.BlockSpec((1, s, LANES), lambda bi, hp, i: (bi, 0, v_blk + hp)),
            pl.BlockSpec((4, DIFF_QK_DIM), lambda bi, hp, i: (0, 0)),
            pl.BlockSpec((1, LANES), lambda bi, hp, i: (0, 0)),
        ],
        out_specs=pl.BlockSpec((1, DIFF_TQ, LANES), lambda bi, hp, i: (bi, i, hp)),
        out_shape=jax.ShapeDtypeStruct((b, s, DIFF_WIDTH), BF16),
        scratch_shapes=[
            pltpu.VMEM((s // DIFF_TK, LANES, DIFF_TK), BF16),
            pltpu.VMEM((4, LANES, DIFF_TQ), F32),
        ],
        compiler_params=_params(("arbitrary", "arbitrary", "arbitrary")),
        name="diff",
    )(ab, ab, ab, lam_vecs.astype(F32), g_tile)


def _dil_kernel(q_ref, k_ref, v_ref, o_ref, o_scr, m_scr, l_scr, *, seq):
    tile = pl.program_id(2)
    hw = DIL_HALF_WIDTH
    lane = lax.broadcasted_iota(jnp.int32, (hw, LANES), 1)
    first = lane < HEAD_DIM
    row = lax.broadcasted_iota(jnp.int32, (2 * hw, 3 * hw), 0) & (hw - 1)
    col = lax.broadcasted_iota(jnp.int32, (2 * hw, 3 * hw), 1)
    delta = col - hw - row
    band = jnp.minimum(delta + hw, hw - delta)

    for pat, (_, dil) in enumerate(DIL_PATTERNS):
        blocks_per_tile = DIL_TILE // (hw * dil)
        n_blocks = seq // (hw * dil)
        units = [(u % dil, u // dil) for u in range(DIL_TILE // hw)]
        for g0 in range(0, len(units), DIL_GROUP):
            group = units[g0:g0 + DIL_GROUP]
            scores, vcats = [], []
            for res, lb in group:
                blk = tile * blocks_per_tile + lb
                q = q_ref[0, pl.ds(lb * (hw * dil) + res, hw, stride=dil), :].astype(BF16)
                zero = jnp.zeros_like(q)
                q2 = jnp.concatenate([jnp.where(first, q, zero), jnp.where(first, zero, q)], axis=0)
                ks, vs = [], []
                for j in (-1, 0, 1):
                    kb = jnp.clip(blk + j, 0, n_blocks - 1)
                    k_start = kb * (hw * dil) + res
                    ks.append(k_ref[0, pl.ds(k_start, hw, stride=dil), :])
                    vs.append(v_ref[0, pl.ds(k_start, hw, stride=dil), :])
                kcat = jnp.concatenate(ks, axis=0).astype(BF16)
                vcats.append(jnp.concatenate(vs, axis=0).astype(BF16))
                s = lax.dot_general(q2, kcat, _NT_DIMS, preferred_element_type=F32)
                col_min = jnp.where(blk > 0, 0, hw)
                col_max = jnp.where(blk < n_blocks - 1, 3 * hw - 1, 2 * hw - 1)
                valid = jnp.minimum(band, jnp.minimum(col - col_min, col_max - col)) >= 0
                scores.append(jnp.where(valid, s, NEG_INF))
            probs, stats = [], []
            for s in scores:
                m = jnp.max(s, axis=1, keepdims=True)
                p = jnp.exp(s - m)
                stats.append((m, jnp.sum(p, axis=1, keepdims=True)))
                probs.append(p.astype(BF16))
            for (res, lb), p, vcat, (m, l) in zip(group, probs, vcats, stats):
                o2 = jnp.dot(p, vcat, preferred_element_type=F32)
                rows = pl.ds(lb * (hw * dil) + res, hw, stride=dil)
                o_scr[pat, rows, :] = jnp.where(first, o2[:hw], o2[hw:])
                m_scr[pat, rows, :] = jnp.where(first, m[:hw], m[hw:])
                l_scr[pat, rows, :] = jnp.where(first, l[:hw], l[hw:])

    m_all = jnp.maximum(jnp.maximum(m_scr[0], m_scr[1]), m_scr[2])
    num = jnp.zeros((DIL_TILE, LANES), F32)
    den = jnp.zeros((DIL_TILE, LANES), F32)
    for pat in range(len(DIL_PATTERNS)):
        w = jnp.exp(m_scr[pat] - m_all)
        num = num + w * o_scr[pat]
        den = den + w * l_scr[pat]
    o_ref[0] = (num / den).astype(o_ref.dtype)


def _dil(c):
    b, s, _ = c.shape
    k_blk = DIL_WIDTH // LANES
    n_pat = len(DIL_PATTERNS)
    return pl.pallas_call(
        functools.partial(_dil_kernel, seq=s),
        grid=(b, DIL_HEADS // 2, s // DIL_TILE),
        in_specs=[
            pl.BlockSpec((1, DIL_TILE, LANES), lambda bi, hp, i: (bi, i, hp)),
            pl.BlockSpec((1, s, LANES), lambda bi, hp, i: (bi, 0, k_blk + hp)),
            pl.BlockSpec((1, s, LANES), lambda bi, hp, i: (bi, 0, 2 * k_blk + hp)),
        ],
        out_specs=pl.BlockSpec((1, DIL_TILE, LANES), lambda bi, hp, i: (bi, i, hp)),
        out_shape=jax.ShapeDtypeStruct((b, s, DIL_WIDTH), BF16),
        scratch_shapes=[
            pltpu.VMEM((n_pat, DIL_TILE, LANES), F32),
            pltpu.VMEM((n_pat, DIL_TILE, LANES), F32),
            pltpu.VMEM((n_pat, DIL_TILE, LANES), F32),
        ],
        compiler_params=_params(("arbitrary", "arbitrary", "arbitrary")),
        name="dil",
    )(c, c, c)


def _layer_norm(z, g, b):
    mu = jnp.mean(z, axis=1, keepdims=True)
    zc = z - mu
    var = jnp.mean(zc * zc, axis=1, keepdims=True)
    return zc * lax.rsqrt(var + LN_EPS) * g + b


def _outproj_kernel(oa_ref, ob_ref, oc_ref, x_ref, mod_ref, w_ref, g_ref, b_ref, y_ref):
    mix = jnp.dot(oa_ref[0], w_ref[0:NA_WIDTH, :], preferred_element_type=F32)
    mix = mix + jnp.dot(ob_ref[0], w_ref[NA_WIDTH:NA_WIDTH + DIFF_WIDTH, :], preferred_element_type=F32)
    mix = mix + jnp.dot(oc_ref[0], w_ref[NA_WIDTH + DIFF_WIDTH:MIX_WIDTH, :], preferred_element_type=F32)
    g1 = mod_ref[0, 2:3, :]
    z = DEEPNORM_ALPHA * x_ref[0] + g1 * mix
    y_ref[0] = _layer_norm(z, g_ref[...], b_ref[...])


def _outproj(oa, ob, oc, x, mod, w_out_bf16, ln_g, ln_b):
    b, s, _ = x.shape
    tm = ROW_TILE
    row = lambda width: pl.BlockSpec((1, tm, width), lambda bi, si: (bi, si, 0))
    vec = pl.BlockSpec((1, D_MODEL), lambda bi, si: (0, 0))
    return pl.pallas_call(
        _outproj_kernel,
        grid=(b, s // tm),
        in_specs=[
            row(NA_WIDTH), row(DIFF_WIDTH), row(DIL_WIDTH), row(D_MODEL),
            pl.BlockSpec((1, 6, D_MODEL), lambda bi, si: (bi, 0, 0)),
            _resident((MIX_WIDTH, D_MODEL), lambda bi, si: (0, 0)),
            vec, vec,
        ],
        out_specs=row(D_MODEL),
        out_shape=jax.ShapeDtypeStruct((b, s, D_MODEL), F32),
        compiler_params=_params(("arbitrary", "arbitrary")),
        name="outproj",
    )(oa, ob, oc, x, mod, w_out_bf16, ln_g.reshape(1, D_MODEL), ln_b.reshape(1, D_MODEL))


def _ffn_kernel(x_ref, mod_ref, wgu_ref, wd_ref, g_ref, b_ref, y_ref):
    x = x_ref[0]
    sh2 = mod_ref[0, 3:4, :]
    sc2 = mod_ref[0, 4:5, :]
    g2 = mod_ref[0, 5:6, :]
    h = (x * (1.0 + sc2) + sh2).astype(BF16)
    acc = jnp.zeros((x.shape[0], D_MODEL), F32)
    for c0, c1 in FFN_CHUNKS:
        gate = jnp.dot(h, wgu_ref[:, c0:c1], preferred_element_type=F32)
        up = jnp.dot(h, wgu_ref[:, FFN_HIDDEN + c0:FFN_HIDDEN + c1], preferred_element_type=F32)
        act = (gate / (1.0 + jnp.exp(-gate)) * up).astype(BF16)
        acc = acc + jnp.dot(act, wd_ref[c0:c1, :], preferred_element_type=F32)
    z = DEEPNORM_ALPHA * x + g2 * acc
    y_ref[0] = _layer_norm(z, g_ref[...], b_ref[...])


def _ffn(x, mod, w_gu_bf16, w_down_bf16, ln_g, ln_b):
    b, s, _ = x.shape
    tm = ROW_TILE
    row = pl.BlockSpec((1, tm, D_MODEL), lambda bi, si: (bi, si, 0))
    vec = pl.BlockSpec((1, D_MODEL), lambda bi, si: (0, 0))
    return pl.pallas_call(
        _ffn_kernel,
        grid=(b, s // tm),
        in_specs=[
            row,
            pl.BlockSpec((1, 6, D_MODEL), lambda bi, si: (bi, 0, 0)),
            _resident((D_MODEL, 2 * FFN_HIDDEN), lambda bi, si: (0, 0)),
            _resident((FFN_HIDDEN, D_MODEL), lambda bi, si: (0, 0)),
            vec, vec,
        ],
        out_specs=row,
        out_shape=jax.ShapeDtypeStruct((b, s, D_MODEL), F32),
        compiler_params=_params(("arbitrary", "arbitrary")),
        name="ffn",
    )(x, mod, w_gu_bf16, w_down_bf16, ln_g.reshape(1, D_MODEL), ln_b.reshape(1, D_MODEL))


def _layer(x, mod, layer, tables, w_in_b, bias_tbl, diff_lambda, diff_subln_g, w_out_b,
           ln1_g, ln1_b, w_gu_b, w_down_b, ln2_g, ln2_b):
    lambda_init = 0.8 - 0.6 * math.exp(-0.3 * layer)
    ab, c = _inproj(x, mod, w_in_b, tables)
    oa = _na(ab, bias_tbl)
    ob = _diff(ab, diff_lambda, diff_subln_g, lambda_init)
    oc = _dil(c)
    x = _outproj(oa, ob, oc, x, mod, w_out_b, ln1_g, ln1_b)
    return _ffn(x, mod, w_gu_b, w_down_b, ln2_g, ln2_b)


def kernel(x_prompt, x_sample, c_prompt, c_sample, w_ada, b_ada, w_in, na_rpb, diff_lambda, diff_subln_g,
           w_out, ln1_g, ln1_b, w_gu, w_down, ln2_g, ln2_b):
    xs = [x_prompt, x_sample]
    n_prompt = c_prompt.shape[0]
    mod_all = _ada(jnp.concatenate([c_prompt, c_sample], axis=0), w_ada, b_ada)
    tables = [_rope_tables(x.shape[1], DIFF_QK_DIM) + _rope_tables(x.shape[1], HEAD_DIM) for x in xs]
    for layer in range(DEPTH):
        w_in_b = w_in[layer].astype(BF16)
        w_out_b = w_out[layer].astype(BF16)
        w_gu_b = w_gu[layer].astype(BF16)
        w_down_b = w_down[layer].astype(BF16)
        bias_tbl = _na_bias_table(na_rpb[layer])
        for g in range(2):
            nb = xs[g].shape[0]
            lo = 0 if g == 0 else n_prompt
            mod = mod_all[layer, lo:lo + nb].reshape(nb, 6, D_MODEL)
            xs[g] = _layer(xs[g], mod, layer, tables[g], w_in_b, bias_tbl, diff_lambda[layer],
                           diff_subln_g[layer], w_out_b, ln1_g[layer], ln1_b[layer], w_gu_b, w_down_b,
                           ln2_g[layer], ln2_b[layer])
    return tuple(xs)
```

```python
import functools
import math

import jax
import jax.numpy as jnp
from jax import lax
from jax.experimental import pallas as pl
from jax.experimental.pallas import tpu as pltpu

F32 = jnp.float32
BF16 = jnp.bfloat16

D_MODEL = 1024
DEPTH = 2
HEAD_DIM = 64
NA_HEADS = 4
DIFF_HEADS = 4
DIL_HEADS = 8
NA_WIDTH = NA_HEADS * HEAD_DIM
DIFF_WIDTH = DIFF_HEADS * HEAD_DIM
DIL_WIDTH = DIL_HEADS * HEAD_DIM
MIX_WIDTH = NA_WIDTH + DIFF_WIDTH + DIL_WIDTH
IN_WIDTH = 3 * MIX_WIDTH
AB_WIDTH = 3 * NA_WIDTH + 3 * DIFF_WIDTH
C_WIDTH = 3 * DIL_WIDTH
GRID_W = 64
NA_WIN_ROWS = 8
NA_WIN_COLS = 16
DIFF_QK_DIM = HEAD_DIM // 2
DIL_PATTERNS = ((128, 1), (512, 4), (2048, 16))
DIL_HALF_WIDTH = 64
FFN_HIDDEN = 2816
ROPE_THETA = 10000.0
LN_EPS = 1e-5
DEEPNORM_ALPHA = (2 * DEPTH) ** 0.25
NEG_INF = -1e30
LOG2E = math.log2(math.e)

LANES = 128
MXU_WIDTH = 256
VMEM_LIMIT_BYTES = 48 * 1024 * 1024

ROW_TILE = 512
NA_ROWS_PER_STEP = 8
DIFF_TQ = 256
DIFF_TK = 512
DIFF_ACC_ROWS = HEAD_DIM + 16
DIL_TILE = 1024
DIL_GROUP = 16
FFN_CHUNKS = ((0, 1024), (1024, 2048), (2048, FFN_HIDDEN))

_NT_DIMS = (((1,), (1,)), ((), ()))


def _params(semantics):
    return pltpu.CompilerParams(dimension_semantics=semantics, vmem_limit_bytes=VMEM_LIMIT_BYTES)


def _resident(block_shape, index_map):
    return pl.BlockSpec(block_shape, index_map, pipeline_mode=pl.Buffered(1))


def _ada_kernel(c_ref, w_ref, b_ref, o_ref):
    c = c_ref[...]
    a = c / (1.0 + jnp.exp(-c))
    o_ref[0] = jnp.dot(a, w_ref[0], precision=lax.Precision.HIGHEST,
                       preferred_element_type=F32) + b_ref[0]


def _ada(c_all, w_ada, b_ada):
    nb = c_all.shape[0]
    n_out = w_ada.shape[-1]
    tn = 1536
    return pl.pallas_call(
        _ada_kernel,
        grid=(DEPTH, n_out // tn),
        in_specs=[
            pl.BlockSpec((nb, D_MODEL), lambda l, n: (0, 0)),
            pl.BlockSpec((1, D_MODEL, tn), lambda l, n: (l, 0, n)),
            pl.BlockSpec((1, 1, tn), lambda l, n: (l, 0, n)),
        ],
        out_specs=pl.BlockSpec((1, nb, tn), lambda l, n: (l, 0, n)),
        out_shape=jax.ShapeDtypeStruct((DEPTH, nb, n_out), F32),
        compiler_params=_params(("arbitrary", "arbitrary")),
        name="ada",
    )(c_all, w_ada, b_ada.reshape(DEPTH, 1, n_out))


def _rope_tables(seq, dim):
    half = dim // 2
    inv_freq = ROPE_THETA ** (-jnp.arange(half, dtype=F32) / half)
    ang = jnp.arange(seq, dtype=F32)[:, None] * inv_freq[None, :]
    cos = jnp.cos(ang)
    sin = jnp.sin(ang)
    reps = LANES // dim
    cos_t = jnp.tile(jnp.concatenate([cos, cos], axis=1), (1, reps))
    sin_t = jnp.tile(jnp.concatenate([-sin, sin], axis=1), (1, reps))
    return cos_t, sin_t


def _rope(y, cos, sin_signed, half):
    lane = lax.broadcasted_iota(jnp.int32, y.shape, 1)
    first = (lane & (2 * half - 1)) < half
    partner = jnp.where(first, pltpu.roll(y, LANES - half, axis=1), pltpu.roll(y, half, axis=1))
    return y * cos + partner * sin_signed


_COLUMN_CLASSES = (
    (0, NA_WIDTH, None, HEAD_DIM ** -0.5),
    (NA_WIDTH, 3 * NA_WIDTH, None, None),
    (3 * NA_WIDTH, 3 * NA_WIDTH + DIFF_WIDTH, DIFF_QK_DIM, DIFF_QK_DIM ** -0.5 * LOG2E),
    (3 * NA_WIDTH + DIFF_WIDTH, 3 * NA_WIDTH + 2 * DIFF_WIDTH, DIFF_QK_DIM, None),
    (3 * NA_WIDTH + 2 * DIFF_WIDTH, AB_WIDTH, None, None),
    (AB_WIDTH, AB_WIDTH + DIL_WIDTH, HEAD_DIM, HEAD_DIM ** -0.5 * LOG2E),
    (AB_WIDTH + DIL_WIDTH, AB_WIDTH + 2 * DIL_WIDTH, HEAD_DIM, None),
    (AB_WIDTH + 2 * DIL_WIDTH, IN_WIDTH, None, None),
)


def _column_class(col):
    for lo, hi, rope_dim, scale in _COLUMN_CLASSES:
        if lo <= col < hi:
            return rope_dim, scale
    raise ValueError(col)


def _inproj_kernel(x_ref, mod_ref, w_ref, cos32_ref, sin32_ref, cos64_ref, sin64_ref, ab_ref, c_ref):
    sh1 = mod_ref[0, 0:1, :]
    sc1 = mod_ref[0, 1:2, :]
    h = (x_ref[0] * (1.0 + sc1) + sh1).astype(BF16)
    for n in range(IN_WIDTH // MXU_WIDTH):
        acc = jnp.dot(h, w_ref[:, n * MXU_WIDTH:(n + 1) * MXU_WIDTH], preferred_element_type=F32)
        for part in range(MXU_WIDTH // LANES):
            col = n * MXU_WIDTH + part * LANES
            y = acc[:, part * LANES:(part + 1) * LANES]
            rope_dim, scale = _column_class(col)
            if rope_dim == DIFF_QK_DIM:
                y = _rope(y, cos32_ref[...], sin32_ref[...], rope_dim // 2)
            elif rope_dim == HEAD_DIM:
                y = _rope(y, cos64_ref[...], sin64_ref[...], rope_dim // 2)
            if scale is not None:
                y = y * scale
            if col < AB_WIDTH:
                ab_ref[0, :, col:col + LANES] = y.astype(ab_ref.dtype)
            else:
                c_ref[0, :, col - AB_WIDTH:col - AB_WIDTH + LANES] = y


def _inproj(x, mod, w_in_bf16, tables):
    b, s, _ = x.shape
    tm = ROW_TILE
    table_spec = pl.BlockSpec((tm, LANES), lambda bi, si: (si, 0))
    return pl.pallas_call(
        _inproj_kernel,
        grid=(b, s // tm),
        in_specs=[
            pl.BlockSpec((1, tm, D_MODEL), lambda bi, si: (bi, si, 0)),
            pl.BlockSpec((1, 6, D_MODEL), lambda bi, si: (bi, 0, 0)),
            _resident((D_MODEL, IN_WIDTH), lambda bi, si: (0, 0)),
            table_spec, table_spec, table_spec, table_spec,
        ],
        out_specs=[
            pl.BlockSpec((1, tm, AB_WIDTH), lambda bi, si: (bi, si, 0)),
            pl.BlockSpec((1, tm, C_WIDTH), lambda bi, si: (bi, si, 0)),
        ],
        out_shape=[
            jax.ShapeDtypeStruct((b, s, AB_WIDTH), BF16),
            jax.ShapeDtypeStruct((b, s, C_WIDTH), F32),
        ],
        compiler_params=_params(("arbitrary", "arbitrary")),
        name="inproj",
    )(x, mod, w_in_bf16, *tables)


def _na_bias_table(rpb):
    c_idx = jnp.arange(GRID_W)
    c_start = jnp.clip(c_idx - NA_WIN_COLS // 2, 0, GRID_W - NA_WIN_COLS)
    col_in = (c_idx[None, :] >= c_start[:, None]) & (c_idx[None, :] < c_start[:, None] + NA_WIN_COLS)
    dc = jnp.clip(c_idx[None, :] - c_idx[:, None] + (NA_WIN_COLS - 1), 0, 2 * NA_WIN_COLS - 2)
    v_idx = jnp.arange(NA_WIN_ROWS)
    j_idx = jnp.arange(NA_WIN_ROWS)
    dr = j_idx[None, :] - v_idx[:, None] + (NA_WIN_ROWS - 1)
    bias = rpb.astype(F32)[:, dr][:, :, :, dc]
    bias = jnp.where(col_in[None, None, None], bias, NEG_INF)
    bias = bias.transpose(0, 1, 3, 2, 4)
    bias = bias.reshape(NA_HEADS // 2, 2, NA_WIN_ROWS, GRID_W, NA_WIN_ROWS * GRID_W)
    return bias.transpose(0, 2, 1, 3, 4).reshape(NA_HEADS // 2, NA_WIN_ROWS, 2 * GRID_W, NA_WIN_ROWS * GRID_W)


def _na_kernel(q_ref, k_ref, v_ref, bias_ref, o_ref, *, n_rows):
    step = pl.program_id(2)
    lane = lax.broadcasted_iota(jnp.int32, (GRID_W, LANES), 1)
    first = lane < HEAD_DIM
    win = NA_WIN_ROWS * GRID_W
    scores, vws = [], []
    for rr in range(NA_ROWS_PER_STEP):
        r = step * NA_ROWS_PER_STEP + rr
        row0 = jnp.clip(r - NA_WIN_ROWS // 2, 0, n_rows - NA_WIN_ROWS)
        start = pl.multiple_of(row0 * GRID_W, GRID_W)
        kw = k_ref[0, pl.ds(start, win), :]
        vws.append(v_ref[0, pl.ds(start, win), :])
        q = q_ref[0, rr * GRID_W:(rr + 1) * GRID_W, :]
        zero = jnp.zeros_like(q)
        q2 = jnp.concatenate([jnp.where(first, q, zero), jnp.where(first, zero, q)], axis=0)
        scores.append(lax.dot_general(q2, kw, _NT_DIMS, preferred_element_type=F32) + bias_ref[0, r - row0])
    probs, dens = [], []
    for s in scores:
        p = jnp.exp(s - jnp.max(s, axis=1, keepdims=True))
        dens.append(jnp.sum(p, axis=1, keepdims=True))
        probs.append(p.astype(BF16))
    for rr, (p, l, vw) in enumerate(zip(probs, dens, vws)):
        o2 = jnp.dot(p, vw, preferred_element_type=F32) / l
        o_ref[0, rr * GRID_W:(rr + 1) * GRID_W, :] = jnp.where(first, o2[:GRID_W], o2[GRID_W:]).astype(o_ref.dtype)


def _na(ab, bias_tbl):
    b, s, _ = ab.shape
    n_rows = s // GRID_W
    tq = NA_ROWS_PER_STEP * GRID_W
    k_blk = NA_WIDTH // LANES
    return pl.pallas_call(
        functools.partial(_na_kernel, n_rows=n_rows),
        grid=(b, NA_HEADS // 2, s // tq),
        in_specs=[
            pl.BlockSpec((1, tq, LANES), lambda bi, hp, i: (bi, i, hp)),
            pl.BlockSpec((1, s, LANES), lambda bi, hp, i: (bi, 0, k_blk + hp)),
            pl.BlockSpec((1, s, LANES), lambda bi, hp, i: (bi, 0, 2 * k_blk + hp)),
            pl.BlockSpec((1, NA_WIN_ROWS, 2 * GRID_W, NA_WIN_ROWS * GRID_W), lambda bi, hp, i: (hp, 0, 0, 0)),
        ],
        out_specs=pl.BlockSpec((1, tq, LANES), lambda bi, hp, i: (bi, i, hp)),
        out_shape=jax.ShapeDtypeStruct((b, s, NA_WIDTH), BF16),
        compiler_params=_params(("arbitrary", "arbitrary", "arbitrary")),
        name="na",
    )(ab, ab, ab, bias_tbl)


def _diff_kernel(q_ref, k_ref, v_ref, lam_ref, g_ref, o_ref, vt_scr, acc_scr, s_even, s_odd, *, seq, lambda_init):
    n_kv = seq // DIFF_TK

    @pl.when(pl.program_id(2) == 0)
    def _():
        ones = jnp.ones((DIFF_ACC_ROWS - HEAD_DIM, DIFF_TK), BF16)
        for t in range(n_kv):
            vt = v_ref[0, t * DIFF_TK:(t + 1) * DIFF_TK, :].astype(F32).T.astype(BF16)
            for h in range(2):
                vt_scr[t, h, 0:HEAD_DIM, :] = vt[h * HEAD_DIM:(h + 1) * HEAD_DIM]
                vt_scr[t, h, HEAD_DIM:DIFF_ACC_ROWS, :] = ones

    q = q_ref[0]
    lane = lax.broadcasted_iota(jnp.int32, (DIFF_TQ, LANES), 1)
    group = lax.shift_right_logical(lane, DIFF_QK_DIM.bit_length() - 1)
    qms = [jnp.where(group == c, q, jnp.zeros_like(q)) for c in range(4)]
    acc_scr[...] = jnp.zeros(acc_scr.shape, F32)

    def scores(t, c, dst):
        start = pl.multiple_of(t * DIFF_TK, DIFF_TK)
        dst[c] = lax.dot_general(k_ref[0, pl.ds(start, DIFF_TK), :], qms[c], _NT_DIMS,
                                 preferred_element_type=F32)

    def consume(t, ms, cur, nxt):
        new_ms = []
        for c in range(4):
            if nxt is not None:
                scores(t + 1, c, nxt)
            st = cur[c]
            m_new = jnp.maximum(ms[c], jnp.max(st, axis=0, keepdims=True))
            p = jnp.exp2(st - m_new).astype(BF16)
            alpha = jnp.exp2(ms[c] - m_new)
            acc_scr[c] = alpha * acc_scr[c] + jnp.dot(vt_scr[t, c // 2], p, preferred_element_type=F32)
            new_ms.append(m_new)
        return tuple(new_ms)

    for c in range(4):
        scores(0, c, s_even)

    def pair(jj, ms):
        ms = consume(2 * jj, ms, s_even, s_odd)
        return consume(2 * jj + 1, ms, s_odd, s_even)

    ms = tuple(jnp.full((1, DIFF_TQ), NEG_INF, F32) for _ in range(4))
    ms = lax.fori_loop(0, n_kv // 2 - 1, pair, ms, unroll=2)
    ms = consume(n_kv - 2, ms, s_even, s_odd)
    consume(n_kv - 1, ms, s_odd, None)

    lf = lam_ref[...]
    lam = (jnp.exp(jnp.sum(lf[0:1] * lf[1:2], axis=1, keepdims=True))
           - jnp.exp(jnp.sum(lf[2:3] * lf[3:4], axis=1, keepdims=True)) + lambda_init)
    o = [acc_scr[c, 0:HEAD_DIM, :] / acc_scr[c, HEAD_DIM:HEAD_DIM + 1, :] for c in range(4)]
    normed = []
    for h in range(2):
        d = o[2 * h] - lam * o[2 * h + 1]
        ms_h = jnp.sum(d * d, axis=0, keepdims=True) * (1.0 / HEAD_DIM)
        normed.append(d * lax.rsqrt(ms_h + LN_EPS))
    out = jnp.concatenate(normed, axis=0).T
    o_ref[0] = (out * g_ref[...] * (1.0 - lambda_init)).astype(o_ref.dtype)


def _diff(ab, lam_vecs, subln_g, lambda_init):
    b, s, _ = ab.shape
    q_blk = 3 * NA_WIDTH // LANES
    k_blk = q_blk + DIFF_WIDTH // LANES
    v_blk = k_blk + DIFF_WIDTH // LANES
    g_tile = jnp.tile(subln_g.astype(F32), LANES // HEAD_DIM).reshape(1, LANES)
    return pl.pallas_call(
        functools.partial(_diff_kernel, seq=s, lambda_init=lambda_init),
        grid=(b, DIFF_HEADS // 2, s // DIFF_TQ),
        in_specs=[
            pl.BlockSpec((1, DIFF_TQ, LANES), lambda bi, hp, i: (bi, i, q_blk + hp)),
            pl.BlockSpec((1, s, LANES), lambda bi, hp, i: (bi, 0, k_blk + hp)),
            pl.BlockSpec((1, s, LANES), lambda bi, hp, i: (bi, 0, v_blk + hp)),
            pl.BlockSpec((4, DIFF_QK_DIM), lambda bi, hp, i: (0, 0)),
            pl.BlockSpec((1, LANES), lambda bi, hp, i: (0, 0)),
        ],
        out_specs=pl.BlockSpec((1, DIFF_TQ, LANES), lambda bi, hp, i: (bi, i, hp)),
        out_shape=jax.ShapeDtypeStruct((b, s, DIFF_WIDTH), BF16),
        scratch_shapes=[
            pltpu.VMEM((s // DIFF_TK, 2, DIFF_ACC_ROWS, DIFF_TK), BF16),
            pltpu.VMEM((4, DIFF_ACC_ROWS, DIFF_TQ), F32),
            pltpu.VMEM((4, DIFF_TK, DIFF_TQ), F32),
            pltpu.VMEM((4, DIFF_TK, DIFF_TQ), F32),
        ],
        compiler_params=_params(("arbitrary", "arbitrary", "arbitrary")),
        name="diff",
    )(ab, ab, ab, lam_vecs.astype(F32), g_tile)


def _dil_kernel(q_ref, k_ref, v_ref, o_ref, o_scr, m_scr, l_scr, bias_scr, *, seq):
    tile = pl.program_id(2)
    hw = DIL_HALF_WIDTH
    lane = lax.broadcasted_iota(jnp.int32, (hw, LANES), 1)
    first = lane < HEAD_DIM
    row = lax.broadcasted_iota(jnp.int32, (2 * hw, 3 * hw), 0) & (hw - 1)
    col = lax.broadcasted_iota(jnp.int32, (2 * hw, 3 * hw), 1)
    delta = col - hw - row
    band = jnp.minimum(delta + hw, hw - delta)
    for variant in range(4):
        col_min = hw if variant & 1 else 0
        col_max = (2 * hw if variant & 2 else 3 * hw) - 1
        ok = jnp.minimum(band, jnp.minimum(col - col_min, col_max - col)) >= 0
        bias_scr[variant] = jnp.where(ok, 0.0, NEG_INF)

    for pat, (_, dil) in enumerate(DIL_PATTERNS):
        blocks_per_tile = DIL_TILE // (hw * dil)
        n_blocks = seq // (hw * dil)
        units = [(u % dil, u // dil) for u in range(DIL_TILE // hw)]
        for g0 in range(0, len(units), DIL_GROUP):
            group = units[g0:g0 + DIL_GROUP]
            scores, vcats = [], []
            for res, lb in group:
                blk = tile * blocks_per_tile + lb
                q = q_ref[0, pl.ds(lb * (hw * dil) + res, hw, stride=dil), :].astype(BF16)
                zero = jnp.zeros_like(q)
                q2 = jnp.concatenate([jnp.where(first, q, zero), jnp.where(first, zero, q)], axis=0)
                ks, vs = [], []
                for j in (-1, 0, 1):
                    kb = jnp.clip(blk + j, 0, n_blocks - 1)
                    k_start = kb * (hw * dil) + res
                    ks.append(k_ref[0, pl.ds(k_start, hw, stride=dil), :])
                    vs.append(v_ref[0, pl.ds(k_start, hw, stride=dil), :])
                kcat = jnp.concatenate(ks, axis=0).astype(BF16)
                vcats.append(jnp.concatenate(vs, axis=0).astype(BF16))
                s = lax.dot_general(q2, kcat, _NT_DIMS, preferred_element_type=F32)
                variant = jnp.where(blk > 0, 0, 1) + jnp.where(blk < n_blocks - 1, 0, 2)
                scores.append(s + bias_scr[variant])
            probs, stats = [], []
            for s in scores:
                m = jnp.max(s, axis=1, keepdims=True)
                p = jnp.exp2(s - m)
                stats.append((m, jnp.sum(p, axis=1, keepdims=True)))
                probs.append(p.astype(BF16))
            for (res, lb), p, vcat, (m, l) in zip(group, probs, vcats, stats):
                o2 = jnp.dot(p, vcat, preferred_element_type=F32)
                rows = pl.ds(lb * (hw * dil) + res, hw, stride=dil)
                o_scr[pat, rows, :] = jnp.where(first, o2[:hw], o2[hw:])
                m_scr[pat, rows, :] = jnp.where(first, m[:hw], m[hw:])
                l_scr[pat, rows, :] = jnp.where(first, l[:hw], l[hw:])

    m_all = jnp.maximum(jnp.maximum(m_scr[0], m_scr[1]), m_scr[2])
    num = jnp.zeros((DIL_TILE, LANES), F32)
    den = jnp.zeros((DIL_TILE, LANES), F32)
    for pat in range(len(DIL_PATTERNS)):
        w = jnp.exp2(m_scr[pat] - m_all)
        num = num + w * o_scr[pat]
        den = den + w * l_scr[pat]
    o_ref[0] = (num / den).astype(o_ref.dtype)


def _dil(c):
    b, s, _ = c.shape
    k_blk = DIL_WIDTH // LANES
    n_pat = len(DIL_PATTERNS)
    return pl.pallas_call(
        functools.partial(_dil_kernel, seq=s),
        grid=(b, DIL_HEADS // 2, s // DIL_TILE),
        in_specs=[
            pl.BlockSpec((1, DIL_TILE, LANES), lambda bi, hp, i: (bi, i, hp)),
            pl.BlockSpec((1, s, LANES), lambda bi, hp, i: (bi, 0, k_blk + hp)),
            pl.BlockSpec((1, s, LANES), lambda bi, hp, i: (bi, 0, 2 * k_blk + hp)),
        ],
        out_specs=pl.BlockSpec((1, DIL_TILE, LANES), lambda bi, hp, i: (bi, i, hp)),
        out_shape=jax.ShapeDtypeStruct((b, s, DIL_WIDTH), BF16),
        scratch_shapes=[
            pltpu.VMEM((n_pat, DIL_TILE, LANES), F32),
            pltpu.VMEM((n_pat, DIL_TILE, LANES), F32),
            pltpu.VMEM((n_pat, DIL_TILE, LANES), F32),
            pltpu.VMEM((4, 2 * DIL_HALF_WIDTH, 3 * DIL_HALF_WIDTH), F32),
        ],
        compiler_params=_params(("arbitrary", "arbitrary", "arbitrary")),
        name="dil",
    )(c, c, c)


def _layer_norm(z, g, b):
    mu = jnp.mean(z, axis=1, keepdims=True)
    zc = z - mu
    var = jnp.mean(zc * zc, axis=1, keepdims=True)
    return zc * lax.rsqrt(var + LN_EPS) * g + b


def _outproj_kernel(oa_ref, ob_ref, oc_ref, x_ref, mod_ref, w_ref, g_ref, b_ref, y_ref):
    mix = jnp.dot(oa_ref[0], w_ref[0:NA_WIDTH, :], preferred_element_type=F32)
    mix = mix + jnp.dot(ob_ref[0], w_ref[NA_WIDTH:NA_WIDTH + DIFF_WIDTH, :], preferred_element_type=F32)
    mix = mix + jnp.dot(oc_ref[0], w_ref[NA_WIDTH + DIFF_WIDTH:MIX_WIDTH, :], preferred_element_type=F32)
    g1 = mod_ref[0, 2:3, :]
    z = DEEPNORM_ALPHA * x_ref[0] + g1 * mix
    y_ref[0] = _layer_norm(z, g_ref[...], b_ref[...])


def _outproj(oa, ob, oc, x, mod, w_out_bf16, ln_g, ln_b):
    b, s, _ = x.shape
    tm = ROW_TILE
    row = lambda width: pl.BlockSpec((1, tm, width), lambda bi, si: (bi, si, 0))
    vec = pl.BlockSpec((1, D_MODEL), lambda bi, si: (0, 0))
    return pl.pallas_call(
        _outproj_kernel,
        grid=(b, s // tm),
        in_specs=[
            row(NA_WIDTH), row(DIFF_WIDTH), row(DIL_WIDTH), row(D_MODEL),
            pl.BlockSpec((1, 6, D_MODEL), lambda bi, si: (bi, 0, 0)),
            _resident((MIX_WIDTH, D_MODEL), lambda bi, si: (0, 0)),
            vec, vec,
        ],
        out_specs=row(D_MODEL),
        out_shape=jax.ShapeDtypeStruct((b, s, D_MODEL), F32),
        compiler_params=_params(("arbitrary", "arbitrary")),
        name="outproj",
    )(oa, ob, oc, x, mod, w_out_bf16, ln_g.reshape(1, D_MODEL), ln_b.reshape(1, D_MODEL))


def _ffn_kernel(x_ref, mod_ref, wgu_ref, wd_ref, g_ref, b_ref, y_ref):
    x = x_ref[0]
    sh2 = mod_ref[0, 3:4, :]
    sc2 = mod_ref[0, 4:5, :]
    g2 = mod_ref[0, 5:6, :]
    h = (x * (1.0 + sc2) + sh2).astype(BF16)
    acc = jnp.zeros((x.shape[0], D_MODEL), F32)
    for c0, c1 in FFN_CHUNKS:
        gate = jnp.dot(h, wgu_ref[:, c0:c1], preferred_element_type=F32)
        up = jnp.dot(h, wgu_ref[:, FFN_HIDDEN + c0:FFN_HIDDEN + c1], preferred_element_type=F32)
        act = (gate / (1.0 + jnp.exp(-gate)) * up).astype(BF16)
        acc = acc + jnp.dot(act, wd_ref[c0:c1, :], preferred_element_type=F32)
    z = DEEPNORM_ALPHA * x + g2 * acc
    y_ref[0] = _layer_norm(z, g_ref[...], b_ref[...])


def _ffn(x, mod, w_gu_bf16, w_down_bf16, ln_g, ln_b):
    b, s, _ = x.shape
    tm = ROW_TILE
    row = pl.BlockSpec((1, tm, D_MODEL), lambda bi, si: (bi, si, 0))
    vec = pl.BlockSpec((1, D_MODEL), lambda bi, si: (0, 0))
    return pl.pallas_call(
        _ffn_kernel,
        grid=(b, s // tm),
        in_specs=[
            row,
            pl.BlockSpec((1, 6, D_MODEL), lambda bi, si: (bi, 0, 0)),
            _resident((D_MODEL, 2 * FFN_HIDDEN), lambda bi, si: (0, 0)),
            _resident((FFN_HIDDEN, D_MODEL), lambda bi, si: (0, 0)),
            vec, vec,
        ],
        out_specs=row,
        out_shape=jax.ShapeDtypeStruct((b, s, D_MODEL), F32),
        compiler_params=_params(("arbitrary", "arbitrary")),
        name="ffn",
    )(x, mod, w_gu_bf16, w_down_bf16, ln_g.reshape(1, D_MODEL), ln_b.reshape(1, D_MODEL))


def _layer(x, mod, layer, tables, w_in_b, bias_tbl, diff_lambda, diff_subln_g, w_out_b,
           ln1_g, ln1_b, w_gu_b, w_down_b, ln2_g, ln2_b):
    lambda_init = 0.8 - 0.6 * math.exp(-0.3 * layer)
    ab, c = _inproj(x, mod, w_in_b, tables)
    oa = _na(ab, bias_tbl)
    ob = _diff(ab, diff_lambda, diff_subln_g, lambda_init)
    oc = _dil(c)
    x = _outproj(oa, ob, oc, x, mod, w_out_b, ln1_g, ln1_b)
    return _ffn(x, mod, w_gu_b, w_down_b, ln2_g, ln2_b)


def kernel(x_prompt, x_sample, c_prompt, c_sample, w_ada, b_ada, w_in, na_rpb, diff_lambda, diff_subln_g,
           w_out, ln1_g, ln1_b, w_gu, w_down, ln2_g, ln2_b):
    xs = [x_prompt, x_sample]
    n_prompt = c_prompt.shape[0]
    mod_all = _ada(jnp.concatenate([c_prompt, c_sample], axis=0), w_ada, b_ada)
    tables = [_rope_tables(x.shape[1], DIFF_QK_DIM) + _rope_tables(x.shape[1], HEAD_DIM) for x in xs]
    for layer in range(DEPTH):
        w_in_b = w_in[layer].astype(BF16)
        w_out_b = w_out[layer].astype(BF16)
        w_gu_b = w_gu[layer].astype(BF16)
        w_down_b = w_down[layer].astype(BF16)
        bias_tbl = _na_bias_table(na_rpb[layer])
        for g in range(2):
            nb = xs[g].shape[0]
            lo = 0 if g == 0 else n_prompt
            mod = mod_all[layer, lo:lo + nb].reshape(nb, 6, D_MODEL)
            xs[g] = _layer(xs[g], mod, layer, tables[g], w_in_b, bias_tbl, diff_lambda[layer],
                           diff_subln_g[layer], w_out_b, ln1_g[layer], ln1_b[layer], w_gu_b, w_down_b,
                           ln2_g[layer], ln2_b[layer])
    return tuple(xs)
```

```python
import functools
import math

import jax
import jax.numpy as jnp
from jax import lax
from jax.experimental import pallas as pl
from jax.experimental.pallas import tpu as pltpu

F32 = jnp.float32
BF16 = jnp.bfloat16

D_MODEL = 1024
DEPTH = 2
HEAD_DIM = 64
NA_HEADS = 4
DIFF_HEADS = 4
DIL_HEADS = 8
NA_WIDTH = NA_HEADS * HEAD_DIM
DIFF_WIDTH = DIFF_HEADS * HEAD_DIM
DIL_WIDTH = DIL_HEADS * HEAD_DIM
MIX_WIDTH = NA_WIDTH + DIFF_WIDTH + DIL_WIDTH
IN_WIDTH = 3 * MIX_WIDTH
AB_WIDTH = 3 * NA_WIDTH + 3 * DIFF_WIDTH
C_WIDTH = 3 * DIL_WIDTH
GRID_W = 64
NA_WIN_ROWS = 8
NA_WIN_COLS = 16
DIFF_QK_DIM = HEAD_DIM // 2
DIL_PATTERNS = ((128, 1), (512, 4), (2048, 16))
DIL_HALF_WIDTH = 64
FFN_HIDDEN = 2816
ROPE_THETA = 10000.0
LN_EPS = 1e-5
DEEPNORM_ALPHA = (2 * DEPTH) ** 0.25
NEG_INF = -1e30
LOG2E = math.log2(math.e)

LANES = 128
MXU_WIDTH = 256
VMEM_LIMIT_BYTES = 48 * 1024 * 1024

ROW_TILE = 512
NA_ROWS_PER_STEP = 8
DIFF_TQ = 256
DIFF_TK = 512
DIFF_ACC_ROWS = HEAD_DIM + 16
DIL_TILE = 1024
DIL_GROUP = 16
FFN_CHUNKS = ((0, 1024), (1024, 2048), (2048, FFN_HIDDEN))

_NT_DIMS = (((1,), (1,)), ((), ()))


def _params(semantics):
    return pltpu.CompilerParams(dimension_semantics=semantics, vmem_limit_bytes=VMEM_LIMIT_BYTES)


def _resident(block_shape, index_map):
    return pl.BlockSpec(block_shape, index_map, pipeline_mode=pl.Buffered(1))


def _ada_kernel(c_ref, w_ref, b_ref, o_ref):
    c = c_ref[...]
    a = c / (1.0 + jnp.exp(-c))
    o_ref[0] = jnp.dot(a, w_ref[0], precision=lax.Precision.HIGHEST,
                       preferred_element_type=F32) + b_ref[0]


def _ada(c_all, w_ada, b_ada):
    nb = c_all.shape[0]
    n_out = w_ada.shape[-1]
    tn = 1536
    return pl.pallas_call(
        _ada_kernel,
        grid=(DEPTH, n_out // tn),
        in_specs=[
            pl.BlockSpec((nb, D_MODEL), lambda l, n: (0, 0)),
            pl.BlockSpec((1, D_MODEL, tn), lambda l, n: (l, 0, n)),
            pl.BlockSpec((1, 1, tn), lambda l, n: (l, 0, n)),
        ],
        out_specs=pl.BlockSpec((1, nb, tn), lambda l, n: (l, 0, n)),
        out_shape=jax.ShapeDtypeStruct((DEPTH, nb, n_out), F32),
        compiler_params=_params(("arbitrary", "arbitrary")),
        name="ada",
    )(c_all, w_ada, b_ada.reshape(DEPTH, 1, n_out))


def _rope_tables(seq, dim):
    half = dim // 2
    inv_freq = ROPE_THETA ** (-jnp.arange(half, dtype=F32) / half)
    ang = jnp.arange(seq, dtype=F32)[:, None] * inv_freq[None, :]
    cos = jnp.cos(ang)
    sin = jnp.sin(ang)
    reps = LANES // dim
    cos_t = jnp.tile(jnp.concatenate([cos, cos], axis=1), (1, reps))
    sin_t = jnp.tile(jnp.concatenate([-sin, sin], axis=1), (1, reps))
    return cos_t, sin_t


def _rope(y, cos, sin_signed, half):
    lane = lax.broadcasted_iota(jnp.int32, y.shape, 1)
    first = (lane & (2 * half - 1)) < half
    partner = jnp.where(first, pltpu.roll(y, LANES - half, axis=1), pltpu.roll(y, half, axis=1))
    return y * cos + partner * sin_signed


_COLUMN_CLASSES = (
    (0, NA_WIDTH, None, HEAD_DIM ** -0.5),
    (NA_WIDTH, 3 * NA_WIDTH, None, None),
    (3 * NA_WIDTH, 3 * NA_WIDTH + DIFF_WIDTH, DIFF_QK_DIM, DIFF_QK_DIM ** -0.5 * LOG2E),
    (3 * NA_WIDTH + DIFF_WIDTH, 3 * NA_WIDTH + 2 * DIFF_WIDTH, DIFF_QK_DIM, None),
    (3 * NA_WIDTH + 2 * DIFF_WIDTH, AB_WIDTH, None, None),
    (AB_WIDTH, AB_WIDTH + DIL_WIDTH, HEAD_DIM, HEAD_DIM ** -0.5 * LOG2E),
    (AB_WIDTH + DIL_WIDTH, AB_WIDTH + 2 * DIL_WIDTH, HEAD_DIM, None),
    (AB_WIDTH + 2 * DIL_WIDTH, IN_WIDTH, None, None),
)


def _column_class(col):
    for lo, hi, rope_dim, scale in _COLUMN_CLASSES:
        if lo <= col < hi:
            return rope_dim, scale
    raise ValueError(col)


def _inproj_kernel(x_ref, mod_ref, w_ref, cos32_ref, sin32_ref, cos64_ref, sin64_ref, ab_ref, c_ref):
    sh1 = mod_ref[0, 0:1, :]
    sc1 = mod_ref[0, 1:2, :]
    h = (x_ref[0] * (1.0 + sc1) + sh1).astype(BF16)
    for n in range(IN_WIDTH // MXU_WIDTH):
        acc = jnp.dot(h, w_ref[:, n * MXU_WIDTH:(n + 1) * MXU_WIDTH], preferred_element_type=F32)
        for part in range(MXU_WIDTH // LANES):
            col = n * MXU_WIDTH + part * LANES
            y = acc[:, part * LANES:(part + 1) * LANES]
            rope_dim, scale = _column_class(col)
            if rope_dim == DIFF_QK_DIM:
                y = _rope(y, cos32_ref[...], sin32_ref[...], rope_dim // 2)
            elif rope_dim == HEAD_DIM:
                y = _rope(y, cos64_ref[...], sin64_ref[...], rope_dim // 2)
            if scale is not None:
                y = y * scale
            if col < AB_WIDTH:
                ab_ref[0, :, col:col + LANES] = y.astype(ab_ref.dtype)
            else:
                c_ref[0, :, col - AB_WIDTH:col - AB_WIDTH + LANES] = y


def _inproj(x, mod, w_in_bf16, tables):
    b, s, _ = x.shape
    tm = ROW_TILE
    table_spec = pl.BlockSpec((tm, LANES), lambda bi, si: (si, 0))
    return pl.pallas_call(
        _inproj_kernel,
        grid=(b, s // tm),
        in_specs=[
            pl.BlockSpec((1, tm, D_MODEL), lambda bi, si: (bi, si, 0)),
            pl.BlockSpec((1, 6, D_MODEL), lambda bi, si: (bi, 0, 0)),
            _resident((D_MODEL, IN_WIDTH), lambda bi, si: (0, 0)),
            table_spec, table_spec, table_spec, table_spec,
        ],
        out_specs=[
            pl.BlockSpec((1, tm, AB_WIDTH), lambda bi, si: (bi, si, 0)),
            pl.BlockSpec((1, tm, C_WIDTH), lambda bi, si: (bi, si, 0)),
        ],
        out_shape=[
            jax.ShapeDtypeStruct((b, s, AB_WIDTH), BF16),
            jax.ShapeDtypeStruct((b, s, C_WIDTH), F32),
        ],
        compiler_params=_params(("arbitrary", "arbitrary")),
        name="inproj",
    )(x, mod, w_in_bf16, *tables)


def _na_bias_table(rpb):
    c_idx = jnp.arange(GRID_W)
    c_start = jnp.clip(c_idx - NA_WIN_COLS // 2, 0, GRID_W - NA_WIN_COLS)
    col_in = (c_idx[None, :] >= c_start[:, None]) & (c_idx[None, :] < c_start[:, None] + NA_WIN_COLS)
    dc = jnp.clip(c_idx[None, :] - c_idx[:, None] + (NA_WIN_COLS - 1), 0, 2 * NA_WIN_COLS - 2)
    v_idx = jnp.arange(NA_WIN_ROWS)
    j_idx = jnp.arange(NA_WIN_ROWS)
    dr = j_idx[None, :] - v_idx[:, None] + (NA_WIN_ROWS - 1)
    bias = rpb.astype(F32)[:, dr][:, :, :, dc]
    bias = jnp.where(col_in[None, None, None], bias, NEG_INF)
    bias = bias.transpose(0, 1, 3, 2, 4)
    bias = bias.reshape(NA_HEADS // 2, 2, NA_WIN_ROWS, GRID_W, NA_WIN_ROWS * GRID_W)
    return bias.transpose(0, 2, 1, 3, 4).reshape(NA_HEADS // 2, NA_WIN_ROWS, 2 * GRID_W, NA_WIN_ROWS * GRID_W)


def _na_kernel(q_ref, k_ref, v_ref, bias_ref, o_ref, *, n_rows):
    step = pl.program_id(2)
    lane = lax.broadcasted_iota(jnp.int32, (GRID_W, LANES), 1)
    first = lane < HEAD_DIM
    win = NA_WIN_ROWS * GRID_W
    scores, vws = [], []
    for rr in range(NA_ROWS_PER_STEP):
        r = step * NA_ROWS_PER_STEP + rr
        row0 = jnp.clip(r - NA_WIN_ROWS // 2, 0, n_rows - NA_WIN_ROWS)
        start = pl.multiple_of(row0 * GRID_W, GRID_W)
        kw = k_ref[0, pl.ds(start, win), :]
        vws.append(v_ref[0, pl.ds(start, win), :])
        q = q_ref[0, rr * GRID_W:(rr + 1) * GRID_W, :]
        zero = jnp.zeros_like(q)
        q2 = jnp.concatenate([jnp.where(first, q, zero), jnp.where(first, zero, q)], axis=0)
        scores.append(lax.dot_general(q2, kw, _NT_DIMS, preferred_element_type=F32) + bias_ref[0, r - row0])
    probs, dens = [], []
    for s in scores:
        p = jnp.exp(s - jnp.max(s, axis=1, keepdims=True))
        dens.append(jnp.sum(p, axis=1, keepdims=True))
        probs.append(p.astype(BF16))
    for rr, (p, l, vw) in enumerate(zip(probs, dens, vws)):
        o2 = jnp.dot(p, vw, preferred_element_type=F32) / l
        o_ref[0, rr * GRID_W:(rr + 1) * GRID_W, :] = jnp.where(first, o2[:GRID_W], o2[GRID_W:]).astype(o_ref.dtype)


def _na(ab, bias_tbl):
    b, s, _ = ab.shape
    n_rows = s // GRID_W
    tq = NA_ROWS_PER_STEP * GRID_W
    k_blk = NA_WIDTH // LANES
    return pl.pallas_call(
        functools.partial(_na_kernel, n_rows=n_rows),
        grid=(b, NA_HEADS // 2, s // tq),
        in_specs=[
            pl.BlockSpec((1, tq, LANES), lambda bi, hp, i: (bi, i, hp)),
            pl.BlockSpec((1, s, LANES), lambda bi, hp, i: (bi, 0, k_blk + hp)),
            pl.BlockSpec((1, s, LANES), lambda bi, hp, i: (bi, 0, 2 * k_blk + hp)),
            pl.BlockSpec((1, NA_WIN_ROWS, 2 * GRID_W, NA_WIN_ROWS * GRID_W), lambda bi, hp, i: (hp, 0, 0, 0)),
        ],
        out_specs=pl.BlockSpec((1, tq, LANES), lambda bi, hp, i: (bi, i, hp)),
        out_shape=jax.ShapeDtypeStruct((b, s, NA_WIDTH), BF16),
        compiler_params=_params(("arbitrary", "arbitrary", "arbitrary")),
        name="na",
    )(ab, ab, ab, bias_tbl)


def _diff_kernel(q_ref, k_ref, v_ref, lam_ref, g_ref, o_ref, vt_scr, acc_scr, s_even, s_odd, *, seq, lambda_init):
    n_kv = seq // DIFF_TK

    @pl.when(pl.program_id(2) == 0)
    def _():
        ones = jnp.ones((DIFF_ACC_ROWS - HEAD_DIM, DIFF_TK), BF16)
        for t in range(n_kv):
            vt = v_ref[0, t * DIFF_TK:(t + 1) * DIFF_TK, :].astype(F32).T.astype(BF16)
            for h in range(2):
                vt_scr[t, h, 0:HEAD_DIM, :] = vt[h * HEAD_DIM:(h + 1) * HEAD_DIM]
                vt_scr[t, h, HEAD_DIM:DIFF_ACC_ROWS, :] = ones

    qt = q_ref[0].astype(F32).T
    feat = lax.broadcasted_iota(jnp.int32, (LANES, DIFF_TQ), 0)
    group = lax.shift_right_logical(feat, DIFF_QK_DIM.bit_length() - 1)
    qmts = [jnp.where(group == c, qt, 0.0).astype(BF16) for c in range(4)]
    acc_scr[...] = jnp.zeros(acc_scr.shape, F32)

    def scores(t, c, dst):
        start = pl.multiple_of(t * DIFF_TK, DIFF_TK)
        dst[c] = jnp.dot(k_ref[0, pl.ds(start, DIFF_TK), :], qmts[c], preferred_element_type=F32)

    def consume(t, ms, cur, nxt):
        new_ms = []
        for c in range(4):
            if nxt is not None:
                scores(t + 1, c, nxt)
            st = cur[c]
            m_new = jnp.maximum(ms[c], jnp.max(st, axis=0, keepdims=True))
            p = jnp.exp2(st - m_new).astype(BF16)
            alpha = jnp.exp2(ms[c] - m_new)
            acc_scr[c] = alpha * acc_scr[c] + jnp.dot(vt_scr[t, c // 2], p, preferred_element_type=F32)
            new_ms.append(m_new)
        return tuple(new_ms)

    for c in range(4):
        scores(0, c, s_even)

    ms = tuple(jnp.full((1, DIFF_TQ), NEG_INF, F32) for _ in range(4))
    for t in range(n_kv):
        cur, nxt = (s_even, s_odd) if t % 2 == 0 else (s_odd, s_even)
        ms = consume(t, ms, cur, nxt if t + 1 < n_kv else None)

    lf = lam_ref[...]
    lam = (jnp.exp(jnp.sum(lf[0:1] * lf[1:2], axis=1, keepdims=True))
           - jnp.exp(jnp.sum(lf[2:3] * lf[3:4], axis=1, keepdims=True)) + lambda_init)
    o = [acc_scr[c, 0:HEAD_DIM, :] / acc_scr[c, HEAD_DIM:HEAD_DIM + 1, :] for c in range(4)]
    normed = []
    for h in range(2):
        d = o[2 * h] - lam * o[2 * h + 1]
        ms_h = jnp.sum(d * d, axis=0, keepdims=True) * (1.0 / HEAD_DIM)
        normed.append(d * lax.rsqrt(ms_h + LN_EPS))
    out = jnp.concatenate(normed, axis=0).T
    o_ref[0] = (out * g_ref[...] * (1.0 - lambda_init)).astype(o_ref.dtype)


def _diff(ab, lam_vecs, subln_g, lambda_init):
    b, s, _ = ab.shape
    q_blk = 3 * NA_WIDTH // LANES
    k_blk = q_blk + DIFF_WIDTH // LANES
    v_blk = k_blk + DIFF_WIDTH // LANES
    g_tile = jnp.tile(subln_g.astype(F32), LANES // HEAD_DIM).reshape(1, LANES)
    return pl.pallas_call(
        functools.partial(_diff_kernel, seq=s, lambda_init=lambda_init),
        grid=(b, DIFF_HEADS // 2, s // DIFF_TQ),
        in_specs=[
            pl.BlockSpec((1, DIFF_TQ, LANES), lambda bi, hp, i: (bi, i, q_blk + hp)),
            pl.BlockSpec((1, s, LANES), lambda bi, hp, i: (bi, 0, k_blk + hp)),
            pl.BlockSpec((1, s, LANES), lambda bi, hp, i: (bi, 0, v_blk + hp)),
            pl.BlockSpec((4, DIFF_QK_DIM), lambda bi, hp, i: (0, 0)),
            pl.BlockSpec((1, LANES), lambda bi, hp, i: (0, 0)),
        ],
        out_specs=pl.BlockSpec((1, DIFF_TQ, LANES), lambda bi, hp, i: (bi, i, hp)),
        out_shape=jax.ShapeDtypeStruct((b, s, DIFF_WIDTH), BF16),
        scratch_shapes=[
            pltpu.VMEM((s // DIFF_TK, 2, DIFF_ACC_ROWS, DIFF_TK), BF16),
            pltpu.VMEM((4, DIFF_ACC_ROWS, DIFF_TQ), F32),
            pltpu.VMEM((4, DIFF_TK, DIFF_TQ), F32),
            pltpu.VMEM((4, DIFF_TK, DIFF_TQ), F32),
        ],
        compiler_params=_params(("arbitrary", "arbitrary", "arbitrary")),
        name="diff",
    )(ab, ab, ab, lam_vecs.astype(F32), g_tile)


def _dil_kernel(q_ref, k_ref, v_ref, o_ref, o_scr, m_scr, l_scr, bias_scr, *, seq):
    tile = pl.program_id(2)
    hw = DIL_HALF_WIDTH
    lane = lax.broadcasted_iota(jnp.int32, (hw, LANES), 1)
    first = lane < HEAD_DIM
    row = lax.broadcasted_iota(jnp.int32, (2 * hw, 3 * hw), 0) & (hw - 1)
    col = lax.broadcasted_iota(jnp.int32, (2 * hw, 3 * hw), 1)
    delta = col - hw - row
    band = jnp.minimum(delta + hw, hw - delta)
    for variant in range(4):
        col_min = hw if variant & 1 else 0
        col_max = (2 * hw if variant & 2 else 3 * hw) - 1
        ok = jnp.minimum(band, jnp.minimum(col - col_min, col_max - col)) >= 0
        bias_scr[variant] = jnp.where(ok, 0.0, NEG_INF)

    for pat, (_, dil) in enumerate(DIL_PATTERNS):
        blocks_per_tile = DIL_TILE // (hw * dil)
        n_blocks = seq // (hw * dil)
        units = [(u % dil, u // dil) for u in range(DIL_TILE // hw)]
        for g0 in range(0, len(units), DIL_GROUP):
            group = units[g0:g0 + DIL_GROUP]
            scores, vcats = [], []
            for res, lb in group:
                blk = tile * blocks_per_tile + lb
                q = q_ref[0, pl.ds(lb * (hw * dil) + res, hw, stride=dil), :].astype(BF16)
                zero = jnp.zeros_like(q)
                q2 = jnp.concatenate([jnp.where(first, q, zero), jnp.where(first, zero, q)], axis=0)
                ks, vs = [], []
                for j in (-1, 0, 1):
                    kb = jnp.clip(blk + j, 0, n_blocks - 1)
                    k_start = kb * (hw * dil) + res
                    ks.append(k_ref[0, pl.ds(k_start, hw, stride=dil), :])
                    vs.append(v_ref[0, pl.ds(k_start, hw, stride=dil), :])
                kcat = jnp.concatenate(ks, axis=0).astype(BF16)
                vcats.append(jnp.concatenate(vs, axis=0).astype(BF16))
                s = lax.dot_general(q2, kcat, _NT_DIMS, preferred_element_type=F32)
                variant = jnp.where(blk > 0, 0, 1) + jnp.where(blk < n_blocks - 1, 0, 2)
                scores.append(s + bias_scr[variant])
            probs, stats = [], []
            for s in scores:
                m = jnp.max(s, axis=1, keepdims=True)
                p = jnp.exp2(s - m)
                stats.append((m, jnp.sum(p, axis=1, keepdims=True)))
                probs.append(p.astype(BF16))
            for (res, lb), p, vcat, (m, l) in zip(group, probs, vcats, stats):
                o2 = jnp.dot(p, vcat, preferred_element_type=F32)
                rows = pl.ds(lb * (hw * dil) + res, hw, stride=dil)
                o_scr[pat, rows, :] = jnp.where(first, o2[:hw], o2[hw:])
                m_scr[pat, rows, :] = jnp.where(first, m[:hw], m[hw:])
                l_scr[pat, rows, :] = jnp.where(first, l[:hw], l[hw:])

    m_all = jnp.maximum(jnp.maximum(m_scr[0], m_scr[1]), m_scr[2])
    num = jnp.zeros((DIL_TILE, LANES), F32)
    den = jnp.zeros((DIL_TILE, LANES), F32)
    for pat in range(len(DIL_PATTERNS)):
        w = jnp.exp2(m_scr[pat] - m_all)
        num = num + w * o_scr[pat]
        den = den + w * l_scr[pat]
    o_ref[0] = (num / den).astype(o_ref.dtype)


def _dil(c):
    b, s, _ = c.shape
    k_blk = DIL_WIDTH // LANES
    n_pat = len(DIL_PATTERNS)
    return pl.pallas_call(
        functools.partial(_dil_kernel, seq=s),
        grid=(b, DIL_HEADS // 2, s // DIL_TILE),
        in_specs=[
            pl.BlockSpec((1, DIL_TILE, LANES), lambda bi, hp, i: (bi, i, hp)),
            pl.BlockSpec((1, s, LANES), lambda bi, hp, i: (bi, 0, k_blk + hp)),
            pl.BlockSpec((1, s, LANES), lambda bi, hp, i: (bi, 0, 2 * k_blk + hp)),
        ],
        out_specs=pl.BlockSpec((1, DIL_TILE, LANES), lambda bi, hp, i: (bi, i, hp)),
        out_shape=jax.ShapeDtypeStruct((b, s, DIL_WIDTH), BF16),
        scratch_shapes=[
            pltpu.VMEM((n_pat, DIL_TILE, LANES), F32),
            pltpu.VMEM((n_pat, DIL_TILE, LANES), F32),
            pltpu.VMEM((n_pat, DIL_TILE, LANES), F32),
            pltpu.VMEM((4, 2 * DIL_HALF_WIDTH, 3 * DIL_HALF_WIDTH), F32),
        ],
        compiler_params=_params(("arbitrary", "arbitrary", "arbitrary")),
        name="dil",
    )(c, c, c)


def _layer_norm(z, g, b):
    mu = jnp.mean(z, axis=1, keepdims=True)
    zc = z - mu
    var = jnp.mean(zc * zc, axis=1, keepdims=True)
    return zc * lax.rsqrt(var + LN_EPS) * g + b


def _outproj_kernel(oa_ref, ob_ref, oc_ref, x_ref, mod_ref, w_ref, g_ref, b_ref, y_ref):
    mix = jnp.dot(oa_ref[0], w_ref[0:NA_WIDTH, :], preferred_element_type=F32)
    mix = mix + jnp.dot(ob_ref[0], w_ref[NA_WIDTH:NA_WIDTH + DIFF_WIDTH, :], preferred_element_type=F32)
    mix = mix + jnp.dot(oc_ref[0], w_ref[NA_WIDTH + DIFF_WIDTH:MIX_WIDTH, :], preferred_element_type=F32)
    g1 = mod_ref[0, 2:3, :]
    z = DEEPNORM_ALPHA * x_ref[0] + g1 * mix
    y_ref[0] = _layer_norm(z, g_ref[...], b_ref[...])


def _outproj(oa, ob, oc, x, mod, w_out_bf16, ln_g, ln_b):
    b, s, _ = x.shape
    tm = ROW_TILE
    row = lambda width: pl.BlockSpec((1, tm, width), lambda bi, si: (bi, si, 0))
    vec = pl.BlockSpec((1, D_MODEL), lambda bi, si: (0, 0))
    return pl.pallas_call(
        _outproj_kernel,
        grid=(b, s // tm),
        in_specs=[
            row(NA_WIDTH), row(DIFF_WIDTH), row(DIL_WIDTH), row(D_MODEL),
            pl.BlockSpec((1, 6, D_MODEL), lambda bi, si: (bi, 0, 0)),
            _resident((MIX_WIDTH, D_MODEL), lambda bi, si: (0, 0)),
            vec, vec,
        ],
        out_specs=row(D_MODEL),
        out_shape=jax.ShapeDtypeStruct((b, s, D_MODEL), F32),
        compiler_params=_params(("arbitrary", "arbitrary")),
        name="outproj",
    )(oa, ob, oc, x, mod, w_out_bf16, ln_g.reshape(1, D_MODEL), ln_b.reshape(1, D_MODEL))


def _ffn_kernel(x_ref, mod_ref, wgu_ref, wd_ref, g_ref, b_ref, y_ref):
    x = x_ref[0]
    sh2 = mod_ref[0, 3:4, :]
    sc2 = mod_ref[0, 4:5, :]
    g2 = mod_ref[0, 5:6, :]
    h = (x * (1.0 + sc2) + sh2).astype(BF16)
    acc = jnp.zeros((x.shape[0], D_MODEL), F32)
    for c0, c1 in FFN_CHUNKS:
        gate = jnp.dot(h, wgu_ref[:, c0:c1], preferred_element_type=F32)
        up = jnp.dot(h, wgu_ref[:, FFN_HIDDEN + c0:FFN_HIDDEN + c1], preferred_element_type=F32)
        act = (gate / (1.0 + jnp.exp(-gate)) * up).astype(BF16)
        acc = acc + jnp.dot(act, wd_ref[c0:c1, :], preferred_element_type=F32)
    z = DEEPNORM_ALPHA * x + g2 * acc
    y_ref[0] = _layer_norm(z, g_ref[...], b_ref[...])


def _ffn(x, mod, w_gu_bf16, w_down_bf16, ln_g, ln_b):
    b, s, _ = x.shape
    tm = ROW_TILE
    row = pl.BlockSpec((1, tm, D_MODEL), lambda bi, si: (bi, si, 0))
    vec = pl.BlockSpec((1, D_MODEL), lambda bi, si: (0, 0))
    return pl.pallas_call(
        _ffn_kernel,
        grid=(b, s // tm),
        in_specs=[
            row,
            pl.BlockSpec((1, 6, D_MODEL), lambda bi, si: (bi, 0, 0)),
            _resident((D_MODEL, 2 * FFN_HIDDEN), lambda bi, si: (0, 0)),
            _resident((FFN_HIDDEN, D_MODEL), lambda bi, si: (0, 0)),
            vec, vec,
        ],
        out_specs=row,
        out_shape=jax.ShapeDtypeStruct((b, s, D_MODEL), F32),
        compiler_params=_params(("arbitrary", "arbitrary")),
        name="ffn",
    )(x, mod, w_gu_bf16, w_down_bf16, ln_g.reshape(1, D_MODEL), ln_b.reshape(1, D_MODEL))


def _layer(x, mod, layer, tables, w_in_b, bias_tbl, diff_lambda, diff_subln_g, w_out_b,
           ln1_g, ln1_b, w_gu_b, w_down_b, ln2_g, ln2_b):
    lambda_init = 0.8 - 0.6 * math.exp(-0.3 * layer)
    ab, c = _inproj(x, mod, w_in_b, tables)
    oa = _na(ab, bias_tbl)
    ob = _diff(ab, diff_lambda, diff_subln_g, lambda_init)
    oc = _dil(c)
    x = _outproj(oa, ob, oc, x, mod, w_out_b, ln1_g, ln1_b)
    return _ffn(x, mod, w_gu_b, w_down_b, ln2_g, ln2_b)


def kernel(x_prompt, x_sample, c_prompt, c_sample, w_ada, b_ada, w_in, na_rpb, diff_lambda, diff_subln_g,
           w_out, ln1_g, ln1_b, w_gu, w_down, ln2_g, ln2_b):
    xs = [x_prompt, x_sample]
    n_prompt = c_prompt.shape[0]
    mod_all = _ada(jnp.concatenate([c_prompt, c_sample], axis=0), w_ada, b_ada)
    tables = [_rope_tables(x.shape[1], DIFF_QK_DIM) + _rope_tables(x.shape[1], HEAD_DIM) for x in xs]
    for layer in range(DEPTH):
        w_in_b = w_in[layer].astype(BF16)
        w_out_b = w_out[layer].astype(BF16)
        w_gu_b = w_gu[layer].astype(BF16)
        w_down_b = w_down[layer].astype(BF16)
        bias_tbl = _na_bias_table(na_rpb[layer])
        for g in range(2):
            nb = xs[g].shape[0]
            lo = 0 if g == 0 else n_prompt
            mod = mod_all[layer, lo:lo + nb].reshape(nb, 6, D_MODEL)
            xs[g] = _layer(xs[g], mod, layer, tables[g], w_in_b, bias_tbl, diff_lambda[layer],
                           diff_subln_g[layer], w_out_b, ln1_g[layer], ln1_b[layer], w_gu_b, w_down_b,
                           ln2_g[layer], ln2_b[layer])
    return tuple(xs)
```

```python
import functools
import math

import jax
import jax.numpy as jnp
from jax import lax
from jax.experimental import pallas as pl
from jax.experimental.pallas import tpu as pltpu

F32 = jnp.float32
BF16 = jnp.bfloat16

D_MODEL = 1024
DEPTH = 2
HEAD_DIM = 64
NA_HEADS = 4
DIFF_HEADS = 4
DIL_HEADS = 8
NA_WIDTH = NA_HEADS * HEAD_DIM
DIFF_WIDTH = DIFF_HEADS * HEAD_DIM
DIL_WIDTH = DIL_HEADS * HEAD_DIM
MIX_WIDTH = NA_WIDTH + DIFF_WIDTH + DIL_WIDTH
IN_WIDTH = 3 * MIX_WIDTH
AB_WIDTH = 3 * NA_WIDTH + 3 * DIFF_WIDTH
C_WIDTH = 3 * DIL_WIDTH
GRID_W = 64
NA_WIN_ROWS = 8
NA_WIN_COLS = 16
DIFF_QK_DIM = HEAD_DIM // 2
DIL_PATTERNS = ((128, 1), (512, 4), (2048, 16))
DIL_HALF_WIDTH = 64
FFN_HIDDEN = 2816
ROPE_THETA = 10000.0
LN_EPS = 1e-5
DEEPNORM_ALPHA = (2 * DEPTH) ** 0.25
NEG_INF = -1e30
LOG2E = math.log2(math.e)

LANES = 128
MXU_WIDTH = 256
VMEM_LIMIT_BYTES = 48 * 1024 * 1024

ROW_TILE = 512
OUTPROJ_SUBTILES = 2
NA_ROWS_PER_STEP = 8
DIFF_TQ = 256
DIFF_TK = 512
DIFF_ACC_ROWS = HEAD_DIM + 16
DIL_TILE = 1024
DIL_KEY_LANES = 2 * LANES
DIL_GROUP = 16
FFN_CHUNKS = ((0, 1024), (1024, 2048), (2048, FFN_HIDDEN))

_NT_DIMS = (((1,), (1,)), ((), ()))


def _params(semantics):
    return pltpu.CompilerParams(dimension_semantics=semantics, vmem_limit_bytes=VMEM_LIMIT_BYTES)


def _resident(block_shape, index_map):
    return pl.BlockSpec(block_shape, index_map, pipeline_mode=pl.Buffered(1))


def _ada_kernel(c_ref, w_ref, b_ref, o_ref):
    c = c_ref[...]
    a = c / (1.0 + jnp.exp(-c))
    o_ref[0] = jnp.dot(a, w_ref[0], precision=lax.Precision.HIGHEST,
                       preferred_element_type=F32) + b_ref[0]


def _ada(c_all, w_ada, b_ada):
    nb = c_all.shape[0]
    n_out = w_ada.shape[-1]
    tn = 1536
    return pl.pallas_call(
        _ada_kernel,
        grid=(DEPTH, n_out // tn),
        in_specs=[
            pl.BlockSpec((nb, D_MODEL), lambda l, n: (0, 0)),
            pl.BlockSpec((1, D_MODEL, tn), lambda l, n: (l, 0, n)),
            pl.BlockSpec((1, 1, tn), lambda l, n: (l, 0, n)),
        ],
        out_specs=pl.BlockSpec((1, nb, tn), lambda l, n: (l, 0, n)),
        out_shape=jax.ShapeDtypeStruct((DEPTH, nb, n_out), F32),
        compiler_params=_params(("arbitrary", "arbitrary")),
        name="ada",
    )(c_all, w_ada, b_ada.reshape(DEPTH, 1, n_out))


def _rope_tables(seq, dim):
    half = dim // 2
    inv_freq = ROPE_THETA ** (-jnp.arange(half, dtype=F32) / half)
    ang = jnp.arange(seq, dtype=F32)[:, None] * inv_freq[None, :]
    cos = jnp.cos(ang)
    sin = jnp.sin(ang)
    reps = LANES // dim
    cos_t = jnp.tile(jnp.concatenate([cos, cos], axis=1), (1, reps))
    sin_t = jnp.tile(jnp.concatenate([-sin, sin], axis=1), (1, reps))
    return cos_t, sin_t


def _rope(y, cos, sin_signed, half):
    lane = lax.broadcasted_iota(jnp.int32, y.shape, 1)
    first = (lane & (2 * half - 1)) < half
    partner = jnp.where(first, pltpu.roll(y, LANES - half, axis=1), pltpu.roll(y, half, axis=1))
    return y * cos + partner * sin_signed


_COLUMN_CLASSES = (
    (0, NA_WIDTH, None, HEAD_DIM ** -0.5),
    (NA_WIDTH, 3 * NA_WIDTH, None, None),
    (3 * NA_WIDTH, 3 * NA_WIDTH + DIFF_WIDTH, DIFF_QK_DIM, DIFF_QK_DIM ** -0.5 * LOG2E),
    (3 * NA_WIDTH + DIFF_WIDTH, 3 * NA_WIDTH + 2 * DIFF_WIDTH, DIFF_QK_DIM, None),
    (3 * NA_WIDTH + 2 * DIFF_WIDTH, AB_WIDTH, None, None),
    (AB_WIDTH, AB_WIDTH + DIL_WIDTH, HEAD_DIM, HEAD_DIM ** -0.5 * LOG2E),
    (AB_WIDTH + DIL_WIDTH, AB_WIDTH + 2 * DIL_WIDTH, HEAD_DIM, None),
    (AB_WIDTH + 2 * DIL_WIDTH, IN_WIDTH, None, None),
)


def _column_class(col):
    for lo, hi, rope_dim, scale in _COLUMN_CLASSES:
        if lo <= col < hi:
            return rope_dim, scale
    raise ValueError(col)


def _inproj_kernel(x_ref, mod_ref, w_ref, cos32_ref, sin32_ref, cos64_ref, sin64_ref, ab_ref, c_ref):
    sh1 = mod_ref[0, 0:1, :]
    sc1 = mod_ref[0, 1:2, :]
    h = (x_ref[0] * (1.0 + sc1) + sh1).astype(BF16)
    for n in range(IN_WIDTH // MXU_WIDTH):
        acc = jnp.dot(h, w_ref[:, n * MXU_WIDTH:(n + 1) * MXU_WIDTH], preferred_element_type=F32)
        for part in range(MXU_WIDTH // LANES):
            col = n * MXU_WIDTH + part * LANES
            y = acc[:, part * LANES:(part + 1) * LANES]
            rope_dim, scale = _column_class(col)
            if rope_dim == DIFF_QK_DIM:
                y = _rope(y, cos32_ref[...], sin32_ref[...], rope_dim // 2)
            elif rope_dim == HEAD_DIM:
                y = _rope(y, cos64_ref[...], sin64_ref[...], rope_dim // 2)
            if scale is not None:
                y = y * scale
            if col < AB_WIDTH:
                ab_ref[0, :, col:col + LANES] = y.astype(ab_ref.dtype)
            else:
                c_ref[0, :, col - AB_WIDTH:col - AB_WIDTH + LANES] = y


def _inproj(x, mod, w_in_bf16, tables):
    b, s, _ = x.shape
    tm = ROW_TILE
    table_spec = pl.BlockSpec((tm, LANES), lambda bi, si: (si, 0))
    return pl.pallas_call(
        _inproj_kernel,
        grid=(b, s // tm),
        in_specs=[
            pl.BlockSpec((1, tm, D_MODEL), lambda bi, si: (bi, si, 0)),
            pl.BlockSpec((1, 6, D_MODEL), lambda bi, si: (bi, 0, 0)),
            _resident((D_MODEL, IN_WIDTH), lambda bi, si: (0, 0)),
            table_spec, table_spec, table_spec, table_spec,
        ],
        out_specs=[
            pl.BlockSpec((1, tm, AB_WIDTH), lambda bi, si: (bi, si, 0)),
            pl.BlockSpec((1, tm, C_WIDTH), lambda bi, si: (bi, si, 0)),
        ],
        out_shape=[
            jax.ShapeDtypeStruct((b, s, AB_WIDTH), BF16),
            jax.ShapeDtypeStruct((b, s, C_WIDTH), F32),
        ],
        compiler_params=_params(("arbitrary", "arbitrary")),
        name="inproj",
    )(x, mod, w_in_bf16, *tables)


def _na_bias_table(rpb):
    c_idx = jnp.arange(GRID_W)
    c_start = jnp.clip(c_idx - NA_WIN_COLS // 2, 0, GRID_W - NA_WIN_COLS)
    col_in = (c_idx[None, :] >= c_start[:, None]) & (c_idx[None, :] < c_start[:, None] + NA_WIN_COLS)
    dc = jnp.clip(c_idx[None, :] - c_idx[:, None] + (NA_WIN_COLS - 1), 0, 2 * NA_WIN_COLS - 2)
    v_idx = jnp.arange(NA_WIN_ROWS)
    j_idx = jnp.arange(NA_WIN_ROWS)
    dr = j_idx[None, :] - v_idx[:, None] + (NA_WIN_ROWS - 1)
    bias = rpb.astype(F32)[:, dr][:, :, :, dc]
    bias = jnp.where(col_in[None, None, None], bias, NEG_INF)
    bias = bias.transpose(0, 1, 3, 2, 4)
    bias = bias.reshape(NA_HEADS // 2, 2, NA_WIN_ROWS, GRID_W, NA_WIN_ROWS * GRID_W)
    return bias.transpose(0, 2, 1, 3, 4).reshape(NA_HEADS // 2, NA_WIN_ROWS, 2 * GRID_W, NA_WIN_ROWS * GRID_W)


def _na_kernel(q_ref, k_ref, v_ref, bias_ref, o_ref, *, n_rows):
    step = pl.program_id(2)
    lane = lax.broadcasted_iota(jnp.int32, (GRID_W, LANES), 1)
    first = lane < HEAD_DIM
    win = NA_WIN_ROWS * GRID_W
    scores, vws = [], []
    for rr in range(NA_ROWS_PER_STEP):
        r = step * NA_ROWS_PER_STEP + rr
        row0 = jnp.clip(r - NA_WIN_ROWS // 2, 0, n_rows - NA_WIN_ROWS)
        start = pl.multiple_of(row0 * GRID_W, GRID_W)
        kw = k_ref[0, pl.ds(start, win), :]
        vws.append(v_ref[0, pl.ds(start, win), :])
        q = q_ref[0, rr * GRID_W:(rr + 1) * GRID_W, :]
        zero = jnp.zeros_like(q)
        q2 = jnp.concatenate([jnp.where(first, q, zero), jnp.where(first, zero, q)], axis=0)
        scores.append(lax.dot_general(q2, kw, _NT_DIMS, preferred_element_type=F32) + bias_ref[0, r - row0])
    probs, dens = [], []
    for s in scores:
        p = jnp.exp(s - jnp.max(s, axis=1, keepdims=True))
        dens.append(jnp.sum(p, axis=1, keepdims=True))
        probs.append(p.astype(BF16))
    for rr, (p, l, vw) in enumerate(zip(probs, dens, vws)):
        o2 = jnp.dot(p, vw, preferred_element_type=F32) / l
        o_ref[0, rr * GRID_W:(rr + 1) * GRID_W, :] = jnp.where(first, o2[:GRID_W], o2[GRID_W:]).astype(o_ref.dtype)


def _na(ab, bias_tbl):
    b, s, _ = ab.shape
    n_rows = s // GRID_W
    tq = NA_ROWS_PER_STEP * GRID_W
    k_blk = NA_WIDTH // LANES
    return pl.pallas_call(
        functools.partial(_na_kernel, n_rows=n_rows),
        grid=(b, NA_HEADS // 2, s // tq),
        in_specs=[
            pl.BlockSpec((1, tq, LANES), lambda bi, hp, i: (bi, i, hp)),
            pl.BlockSpec((1, s, LANES), lambda bi, hp, i: (bi, 0, k_blk + hp)),
            pl.BlockSpec((1, s, LANES), lambda bi, hp, i: (bi, 0, 2 * k_blk + hp)),
            pl.BlockSpec((1, NA_WIN_ROWS, 2 * GRID_W, NA_WIN_ROWS * GRID_W), lambda bi, hp, i: (hp, 0, 0, 0)),
        ],
        out_specs=pl.BlockSpec((1, tq, LANES), lambda bi, hp, i: (bi, i, hp)),
        out_shape=jax.ShapeDtypeStruct((b, s, NA_WIDTH), BF16),
        compiler_params=_params(("arbitrary", "arbitrary", "arbitrary")),
        name="na",
    )(ab, ab, ab, bias_tbl)


def _diff_kernel(q_ref, k_ref, v_ref, lam_ref, g_ref, o_ref, vt_scr, acc_scr, s_even, s_odd, qm_scr, *,
                 seq, lambda_init):
    n_kv = seq // DIFF_TK
    n_q = seq // DIFF_TQ
    assert n_kv % 2 == 0, "the last kv tile must read s_odd so that s_even is free for the next query tile"
    q_tile = pl.program_id(2)
    feat = lax.broadcasted_iota(jnp.int32, (LANES, DIFF_TQ), 0)
    group = lax.shift_right_logical(feat, DIFF_QK_DIM.bit_length() - 1)

    def masked_qt(tile_idx):
        start = pl.multiple_of(tile_idx * DIFF_TQ, DIFF_TQ)
        qt = q_ref[0, pl.ds(start, DIFF_TQ), :].astype(F32).T
        return [jnp.where(group == c, qt, 0.0).astype(BF16) for c in range(4)]

    def scores(t, c, qm, dst):
        start = pl.multiple_of(t * DIFF_TK, DIFF_TK)
        dst[c] = jnp.dot(k_ref[0, pl.ds(start, DIFF_TK), :], qm, preferred_element_type=F32)

    @pl.when(q_tile == 0)
    def _():
        ones = jnp.ones((DIFF_ACC_ROWS - HEAD_DIM, DIFF_TK), BF16)
        for t in range(n_kv):
            vt = v_ref[0, t * DIFF_TK:(t + 1) * DIFF_TK, :].astype(F32).T.astype(BF16)
            for h in range(2):
                vt_scr[t, h, 0:HEAD_DIM, :] = vt[h * HEAD_DIM:(h + 1) * HEAD_DIM]
                vt_scr[t, h, HEAD_DIM:DIFF_ACC_ROWS, :] = ones
        for c, qm in enumerate(masked_qt(0)):
            qm_scr[c] = qm
            scores(0, c, qm, s_even)

    qmts = [qm_scr[c] for c in range(4)]
    acc_scr[...] = jnp.zeros(acc_scr.shape, F32)

    def consume(t, ms, cur, nxt):
        last = t + 1 == n_kv
        if last:
            next_qms = masked_qt(jnp.minimum(q_tile + 1, n_q - 1))
        new_ms = []
        for c in range(4):
            if last:
                qm_scr[c] = next_qms[c]
                scores(0, c, next_qms[c], nxt)
            else:
                scores(t + 1, c, qmts[c], nxt)
            st = cur[c]
            m_new = jnp.maximum(ms[c], jnp.max(st, axis=0, keepdims=True))
            p = jnp.exp2(st - m_new).astype(BF16)
            alpha = jnp.exp2(ms[c] - m_new)
            acc_scr[c] = alpha * acc_scr[c] + jnp.dot(vt_scr[t, c // 2], p, preferred_element_type=F32)
            new_ms.append(m_new)
        return tuple(new_ms)

    ms = tuple(jnp.full((1, DIFF_TQ), NEG_INF, F32) for _ in range(4))
    for t in range(n_kv):
        cur, nxt = (s_even, s_odd) if t % 2 == 0 else (s_odd, s_even)
        ms = consume(t, ms, cur, nxt)

    lf = lam_ref[...]
    lam = (jnp.exp(jnp.sum(lf[0:1] * lf[1:2], axis=1, keepdims=True))
           - jnp.exp(jnp.sum(lf[2:3] * lf[3:4], axis=1, keepdims=True)) + lambda_init)
    o = [acc_scr[c, 0:HEAD_DIM, :] / acc_scr[c, HEAD_DIM:HEAD_DIM + 1, :] for c in range(4)]
    normed = []
    for h in range(2):
        d = o[2 * h] - lam * o[2 * h + 1]
        ms_h = jnp.sum(d * d, axis=0, keepdims=True) * (1.0 / HEAD_DIM)
        normed.append(d * lax.rsqrt(ms_h + LN_EPS))
    out = jnp.concatenate(normed, axis=0).T
    o_ref[0] = (out * g_ref[...] * (1.0 - lambda_init)).astype(o_ref.dtype)


def _diff(ab, lam_vecs, subln_g, lambda_init):
    b, s, _ = ab.shape
    q_blk = 3 * NA_WIDTH // LANES
    k_blk = q_blk + DIFF_WIDTH // LANES
    v_blk = k_blk + DIFF_WIDTH // LANES
    g_tile = jnp.tile(subln_g.astype(F32), LANES // HEAD_DIM).reshape(1, LANES)
    return pl.pallas_call(
        functools.partial(_diff_kernel, seq=s, lambda_init=lambda_init),
        grid=(b, DIFF_HEADS // 2, s // DIFF_TQ),
        in_specs=[
            pl.BlockSpec((1, s, LANES), lambda bi, hp, i: (bi, 0, q_blk + hp)),
            pl.BlockSpec((1, s, LANES), lambda bi, hp, i: (bi, 0, k_blk + hp)),
            pl.BlockSpec((1, s, LANES), lambda bi, hp, i: (bi, 0, v_blk + hp)),
            pl.BlockSpec((4, DIFF_QK_DIM), lambda bi, hp, i: (0, 0)),
            pl.BlockSpec((1, LANES), lambda bi, hp, i: (0, 0)),
        ],
        out_specs=pl.BlockSpec((1, DIFF_TQ, LANES), lambda bi, hp, i: (bi, i, hp)),
        out_shape=jax.ShapeDtypeStruct((b, s, DIFF_WIDTH), BF16),
        scratch_shapes=[
            pltpu.VMEM((s // DIFF_TK, 2, DIFF_ACC_ROWS, DIFF_TK), BF16),
            pltpu.VMEM((4, DIFF_ACC_ROWS, DIFF_TQ), F32),
            pltpu.VMEM((4, DIFF_TK, DIFF_TQ), F32),
            pltpu.VMEM((4, DIFF_TK, DIFF_TQ), F32),
            pltpu.VMEM((4, LANES, DIFF_TQ), BF16),
        ],
        compiler_params=_params(("arbitrary", "arbitrary", "arbitrary")),
        name="diff",
    )(ab, ab, ab, lam_vecs.astype(F32), g_tile)


def _dil_kernel(q_ref, k_ref, v_ref, o_ref, o_scr, m_scr, l_scr, bias_scr, *, seq):
    tile = pl.program_id(2)
    hw = DIL_HALF_WIDTH
    lane = lax.broadcasted_iota(jnp.int32, (hw, LANES), 1)
    first = lane < HEAD_DIM
    row = lax.broadcasted_iota(jnp.int32, (2 * hw, DIL_KEY_LANES), 0) & (hw - 1)
    col = lax.broadcasted_iota(jnp.int32, (2 * hw, DIL_KEY_LANES), 1)
    pad = jnp.zeros((DIL_KEY_LANES - 3 * hw, LANES), BF16)
    delta = col - hw - row
    band = jnp.minimum(delta + hw, hw - delta)
    for variant in range(4):
        col_min = hw if variant & 1 else 0
        col_max = (2 * hw if variant & 2 else 3 * hw) - 1
        ok = jnp.minimum(band, jnp.minimum(col - col_min, col_max - col)) >= 0
        bias_scr[variant] = jnp.where(ok, 0.0, NEG_INF)

    for pat, (_, dil) in enumerate(DIL_PATTERNS):
        blocks_per_tile = DIL_TILE // (hw * dil)
        n_blocks = seq // (hw * dil)
        units = [(u % dil, u // dil) for u in range(DIL_TILE // hw)]
        for g0 in range(0, len(units), DIL_GROUP):
            group = units[g0:g0 + DIL_GROUP]
            scores, vcats = [], []
            for res, lb in group:
                blk = tile * blocks_per_tile + lb
                q = q_ref[0, pl.ds(lb * (hw * dil) + res, hw, stride=dil), :].astype(BF16)
                zero = jnp.zeros_like(q)
                q2 = jnp.concatenate([jnp.where(first, q, zero), jnp.where(first, zero, q)], axis=0)
                ks, vs = [], []
                for j in (-1, 0, 1):
                    kb = jnp.clip(blk + j, 0, n_blocks - 1)
                    k_start = kb * (hw * dil) + res
                    ks.append(k_ref[0, pl.ds(k_start, hw, stride=dil), :])
                    vs.append(v_ref[0, pl.ds(k_start, hw, stride=dil), :])
                kcat = jnp.concatenate([jnp.concatenate(ks, axis=0).astype(BF16), pad], axis=0)
                vcats.append(jnp.concatenate([jnp.concatenate(vs, axis=0).astype(BF16), pad], axis=0))
                s = lax.dot_general(q2, kcat, _NT_DIMS, preferred_element_type=F32)
                variant = jnp.where(blk > 0, 0, 1) + jnp.where(blk < n_blocks - 1, 0, 2)
                scores.append(s + bias_scr[variant])
            probs, stats = [], []
            for s in scores:
                m = jnp.max(s, axis=1, keepdims=True)
                p = jnp.exp2(s - m)
                stats.append((m, jnp.sum(p, axis=1, keepdims=True)))
                probs.append(p.astype(BF16))
            for (res, lb), p, vcat, (m, l) in zip(group, probs, vcats, stats):
                o2 = jnp.dot(p, vcat, preferred_element_type=F32)
                rows = pl.ds(lb * (hw * dil) + res, hw, stride=dil)
                o_scr[pat, rows, :] = jnp.where(first, o2[:hw], o2[hw:])
                m_scr[pat, rows, :] = jnp.where(first, m[:hw], m[hw:])
                l_scr[pat, rows, :] = jnp.where(first, l[:hw], l[hw:])

    m_all = jnp.maximum(jnp.maximum(m_scr[0], m_scr[1]), m_scr[2])
    num = jnp.zeros((DIL_TILE, LANES), F32)
    den = jnp.zeros((DIL_TILE, LANES), F32)
    for pat in range(len(DIL_PATTERNS)):
        w = jnp.exp2(m_scr[pat] - m_all)
        num = num + w * o_scr[pat]
        den = den + w * l_scr[pat]
    o_ref[0] = (num / den).astype(o_ref.dtype)


def _dil(c):
    b, s, _ = c.shape
    k_blk = DIL_WIDTH // LANES
    n_pat = len(DIL_PATTERNS)
    return pl.pallas_call(
        functools.partial(_dil_kernel, seq=s),
        grid=(b, DIL_HEADS // 2, s // DIL_TILE),
        in_specs=[
            pl.BlockSpec((1, DIL_TILE, LANES), lambda bi, hp, i: (bi, i, hp)),
            pl.BlockSpec((1, s, LANES), lambda bi, hp, i: (bi, 0, k_blk + hp)),
            pl.BlockSpec((1, s, LANES), lambda bi, hp, i: (bi, 0, 2 * k_blk + hp)),
        ],
        out_specs=pl.BlockSpec((1, DIL_TILE, LANES), lambda bi, hp, i: (bi, i, hp)),
        out_shape=jax.ShapeDtypeStruct((b, s, DIL_WIDTH), BF16),
        scratch_shapes=[
            pltpu.VMEM((n_pat, DIL_TILE, LANES), F32),
            pltpu.VMEM((n_pat, DIL_TILE, LANES), F32),
            pltpu.VMEM((n_pat, DIL_TILE, LANES), F32),
            pltpu.VMEM((4, 2 * DIL_HALF_WIDTH, DIL_KEY_LANES), F32),
        ],
        compiler_params=_params(("arbitrary", "arbitrary", "arbitrary")),
        name="dil",
    )(c, c, c)


def _layer_norm(z, g, b):
    mu = jnp.mean(z, axis=1, keepdims=True)
    zc = z - mu
    var = jnp.mean(zc * zc, axis=1, keepdims=True)
    return zc * lax.rsqrt(var + LN_EPS) * g + b


def _outproj_kernel(oa_ref, ob_ref, oc_ref, x_ref, mod_ref, w_ref, g_ref, b_ref, y_ref):
    g1 = mod_ref[0, 2:3, :]
    for sub in range(OUTPROJ_SUBTILES):
        rows = slice(sub * ROW_TILE, (sub + 1) * ROW_TILE)
        mix = jnp.dot(oa_ref[0, rows, :], w_ref[0:NA_WIDTH, :], preferred_element_type=F32)
        mix = mix + jnp.dot(ob_ref[0, rows, :], w_ref[NA_WIDTH:NA_WIDTH + DIFF_WIDTH, :],
                            preferred_element_type=F32)
        mix = mix + jnp.dot(oc_ref[0, rows, :], w_ref[NA_WIDTH + DIFF_WIDTH:MIX_WIDTH, :],
                            preferred_element_type=F32)
        z = DEEPNORM_ALPHA * x_ref[0, rows, :] + g1 * mix
        y_ref[0, rows, :] = _layer_norm(z, g_ref[...], b_ref[...])


def _outproj(oa, ob, oc, x, mod, w_out_bf16, ln_g, ln_b):
    b, s, _ = x.shape
    tm = OUTPROJ_SUBTILES * ROW_TILE
    row = lambda width: pl.BlockSpec((1, tm, width), lambda bi, si: (bi, si, 0))
    vec = pl.BlockSpec((1, D_MODEL), lambda bi, si: (0, 0))
    return pl.pallas_call(
        _outproj_kernel,
        grid=(b, s // tm),
        in_specs=[
            row(NA_WIDTH), row(DIFF_WIDTH), row(DIL_WIDTH), row(D_MODEL),
            pl.BlockSpec((1, 6, D_MODEL), lambda bi, si: (bi, 0, 0)),
            _resident((MIX_WIDTH, D_MODEL), lambda bi, si: (0, 0)),
            vec, vec,
        ],
        out_specs=row(D_MODEL),
        out_shape=jax.ShapeDtypeStruct((b, s, D_MODEL), F32),
        compiler_params=_params(("arbitrary", "arbitrary")),
        name="outproj",
    )(oa, ob, oc, x, mod, w_out_bf16, ln_g.reshape(1, D_MODEL), ln_b.reshape(1, D_MODEL))


def _ffn_kernel(x_ref, mod_ref, wgu_ref, wd_ref, g_ref, b_ref, y_ref):
    x = x_ref[0]
    sh2 = mod_ref[0, 3:4, :]
    sc2 = mod_ref[0, 4:5, :]
    g2 = mod_ref[0, 5:6, :]
    h = (x * (1.0 + sc2) + sh2).astype(BF16)
    acc = jnp.zeros((x.shape[0], D_MODEL), F32)
    for c0, c1 in FFN_CHUNKS:
        gate = jnp.dot(h, wgu_ref[:, c0:c1], preferred_element_type=F32)
        up = jnp.dot(h, wgu_ref[:, FFN_HIDDEN + c0:FFN_HIDDEN + c1], preferred_element_type=F32)
        act = (gate / (1.0 + jnp.exp(-gate)) * up).astype(BF16)
        acc = acc + jnp.dot(act, wd_ref[c0:c1, :], preferred_element_type=F32)
    z = DEEPNORM_ALPHA * x + g2 * acc
    y_ref[0] = _layer_norm(z, g_ref[...], b_ref[...])


def _ffn(x, mod, w_gu_bf16, w_down_bf16, ln_g, ln_b):
    b, s, _ = x.shape
    tm = ROW_TILE
    row = pl.BlockSpec((1, tm, D_MODEL), lambda bi, si: (bi, si, 0))
    vec = pl.BlockSpec((1, D_MODEL), lambda bi, si: (0, 0))
    return pl.pallas_call(
        _ffn_kernel,
        grid=(b, s // tm),
        in_specs=[
            row,
            pl.BlockSpec((1, 6, D_MODEL), lambda bi, si: (bi, 0, 0)),
            _resident((D_MODEL, 2 * FFN_HIDDEN), lambda bi, si: (0, 0)),
            _resident((FFN_HIDDEN, D_MODEL), lambda bi, si: (0, 0)),
            vec, vec,
        ],
        out_specs=row,
        out_shape=jax.ShapeDtypeStruct((b, s, D_MODEL), F32),
        compiler_params=_params(("arbitrary", "arbitrary")),
        name="ffn",
    )(x, mod, w_gu_bf16, w_down_bf16, ln_g.reshape(1, D_MODEL), ln_b.reshape(1, D_MODEL))


def _layer(x, mod, layer, tables, w_in_b, bias_tbl, diff_lambda, diff_subln_g, w_out_b,
           ln1_g, ln1_b, w_gu_b, w_down_b, ln2_g, ln2_b):
    lambda_init = 0.8 - 0.6 * math.exp(-0.3 * layer)
    ab, c = _inproj(x, mod, w_in_b, tables)
    oa = _na(ab, bias_tbl)
    ob = _diff(ab, diff_lambda, diff_subln_g, lambda_init)
    oc = _dil(c)
    x = _outproj(oa, ob, oc, x, mod, w_out_b, ln1_g, ln1_b)
    return _ffn(x, mod, w_gu_b, w_down_b, ln2_g, ln2_b)


def kernel(x_prompt, x_sample, c_prompt, c_sample, w_ada, b_ada, w_in, na_rpb, diff_lambda, diff_subln_g,
           w_out, ln1_g, ln1_b, w_gu, w_down, ln2_g, ln2_b):
    xs = [x_prompt, x_sample]
    n_prompt = c_prompt.shape[0]
    mod_all = _ada(jnp.concatenate([c_prompt, c_sample], axis=0), w_ada, b_ada)
    tables = [_rope_tables(x.shape[1], DIFF_QK_DIM) + _rope_tables(x.shape[1], HEAD_DIM) for x in xs]
    for layer in range(DEPTH):
        w_in_b = w_in[layer].astype(BF16)
        w_out_b = w_out[layer].astype(BF16)
        w_gu_b = w_gu[layer].astype(BF16)
        w_down_b = w_down[layer].astype(BF16)
        bias_tbl = _na_bias_table(na_rpb[layer])
        for g in range(2):
            nb = xs[g].shape[0]
            lo = 0 if g == 0 else n_prompt
            mod = mod_all[layer, lo:lo + nb].reshape(nb, 6, D_MODEL)
            xs[g] = _layer(xs[g], mod, layer, tables[g], w_in_b, bias_tbl, diff_lambda[layer],
                           diff_subln_g[layer], w_out_b, ln1_g[layer], ln1_b[layer], w_gu_b, w_down_b,
                           ln2_g[layer], ln2_b[layer])
    return tuple(xs)
```

```python
import functools
import math

import jax
import jax.numpy as jnp
from jax import lax
from jax.experimental import pallas as pl
from jax.experimental.pallas import tpu as pltpu

F32 = jnp.float32
BF16 = jnp.bfloat16

D_MODEL = 1024
DEPTH = 2
HEAD_DIM = 64
NA_HEADS = 4
DIFF_HEADS = 4
DIL_HEADS = 8
NA_WIDTH = NA_HEADS * HEAD_DIM
DIFF_WIDTH = DIFF_HEADS * HEAD_DIM
DIL_WIDTH = DIL_HEADS * HEAD_DIM
MIX_WIDTH = NA_WIDTH + DIFF_WIDTH + DIL_WIDTH
IN_WIDTH = 3 * MIX_WIDTH
AB_WIDTH = 3 * NA_WIDTH + 3 * DIFF_WIDTH
C_WIDTH = 3 * DIL_WIDTH
GRID_W = 64
NA_WIN_ROWS = 8
NA_WIN_COLS = 16
DIFF_QK_DIM = HEAD_DIM // 2
DIL_PATTERNS = ((128, 1), (512, 4), (2048, 16))
DIL_HALF_WIDTH = 64
FFN_HIDDEN = 2816
ROPE_THETA = 10000.0
LN_EPS = 1e-5
DEEPNORM_ALPHA = (2 * DEPTH) ** 0.25
NEG_INF = -1e30
LOG2E = math.log2(math.e)

LANES = 128
MXU_WIDTH = 256
VMEM_LIMIT_BYTES = 48 * 1024 * 1024

ROW_TILE = 512
OUTPROJ_SUBTILES = 2
NA_ROWS_PER_STEP = 16
DIFF_TQ = 256
DIFF_TK = 512
DIFF_ACC_ROWS = HEAD_DIM + 16
DIL_TILE = 1024
DIL_KEY_LANES = 2 * LANES
DIL_GROUP = 16
FFN_CHUNKS = ((0, 1024), (1024, 2048), (2048, FFN_HIDDEN))

_NT_DIMS = (((1,), (1,)), ((), ()))


def _params(semantics):
    return pltpu.CompilerParams(dimension_semantics=semantics, vmem_limit_bytes=VMEM_LIMIT_BYTES)


def _resident(block_shape, index_map):
    return pl.BlockSpec(block_shape, index_map, pipeline_mode=pl.Buffered(1))


def _ada_kernel(c_ref, w_ref, b_ref, o_ref):
    c = c_ref[...]
    a = c / (1.0 + jnp.exp(-c))
    o_ref[0] = jnp.dot(a, w_ref[0], precision=lax.Precision.HIGHEST,
                       preferred_element_type=F32) + b_ref[0]


def _ada(c_all, w_ada, b_ada):
    nb = c_all.shape[0]
    n_out = w_ada.shape[-1]
    tn = 1536
    return pl.pallas_call(
        _ada_kernel,
        grid=(DEPTH, n_out // tn),
        in_specs=[
            pl.BlockSpec((nb, D_MODEL), lambda l, n: (0, 0)),
            pl.BlockSpec((1, D_MODEL, tn), lambda l, n: (l, 0, n)),
            pl.BlockSpec((1, 1, tn), lambda l, n: (l, 0, n)),
        ],
        out_specs=pl.BlockSpec((1, nb, tn), lambda l, n: (l, 0, n)),
        out_shape=jax.ShapeDtypeStruct((DEPTH, nb, n_out), F32),
        compiler_params=_params(("arbitrary", "arbitrary")),
        name="ada",
    )(c_all, w_ada, b_ada.reshape(DEPTH, 1, n_out))


def _rope_tables(seq, dim):
    half = dim // 2
    inv_freq = ROPE_THETA ** (-jnp.arange(half, dtype=F32) / half)
    ang = jnp.arange(seq, dtype=F32)[:, None] * inv_freq[None, :]
    cos = jnp.cos(ang)
    sin = jnp.sin(ang)
    reps = LANES // dim
    cos_t = jnp.tile(jnp.concatenate([cos, cos], axis=1), (1, reps))
    sin_t = jnp.tile(jnp.concatenate([-sin, sin], axis=1), (1, reps))
    return cos_t, sin_t


def _rope(y, cos, sin_signed, half):
    lane = lax.broadcasted_iota(jnp.int32, y.shape, 1)
    first = (lane & (2 * half - 1)) < half
    partner = jnp.where(first, pltpu.roll(y, LANES - half, axis=1), pltpu.roll(y, half, axis=1))
    return y * cos + partner * sin_signed


_COLUMN_CLASSES = (
    (0, NA_WIDTH, None, HEAD_DIM ** -0.5),
    (NA_WIDTH, 3 * NA_WIDTH, None, None),
    (3 * NA_WIDTH, 3 * NA_WIDTH + DIFF_WIDTH, DIFF_QK_DIM, DIFF_QK_DIM ** -0.5 * LOG2E),
    (3 * NA_WIDTH + DIFF_WIDTH, 3 * NA_WIDTH + 2 * DIFF_WIDTH, DIFF_QK_DIM, None),
    (3 * NA_WIDTH + 2 * DIFF_WIDTH, AB_WIDTH, None, None),
    (AB_WIDTH, AB_WIDTH + DIL_WIDTH, HEAD_DIM, HEAD_DIM ** -0.5 * LOG2E),
    (AB_WIDTH + DIL_WIDTH, AB_WIDTH + 2 * DIL_WIDTH, HEAD_DIM, None),
    (AB_WIDTH + 2 * DIL_WIDTH, IN_WIDTH, None, None),
)


def _column_class(col):
    for lo, hi, rope_dim, scale in _COLUMN_CLASSES:
        if lo <= col < hi:
            return rope_dim, scale
    raise ValueError(col)


def _inproj_kernel(x_ref, mod_ref, w_ref, cos32_ref, sin32_ref, cos64_ref, sin64_ref, ab_ref, c_ref):
    sh1 = mod_ref[0, 0:1, :]
    sc1 = mod_ref[0, 1:2, :]
    h = (x_ref[0] * (1.0 + sc1) + sh1).astype(BF16)
    for n in range(IN_WIDTH // MXU_WIDTH):
        acc = jnp.dot(h, w_ref[:, n * MXU_WIDTH:(n + 1) * MXU_WIDTH], preferred_element_type=F32)
        for part in range(MXU_WIDTH // LANES):
            col = n * MXU_WIDTH + part * LANES
            y = acc[:, part * LANES:(part + 1) * LANES]
            rope_dim, scale = _column_class(col)
            if rope_dim == DIFF_QK_DIM:
                y = _rope(y, cos32_ref[...], sin32_ref[...], rope_dim // 2)
            elif rope_dim == HEAD_DIM:
                y = _rope(y, cos64_ref[...], sin64_ref[...], rope_dim // 2)
            if scale is not None:
                y = y * scale
            if col < AB_WIDTH:
                ab_ref[0, :, col:col + LANES] = y.astype(ab_ref.dtype)
            else:
                c_ref[0, :, col - AB_WIDTH:col - AB_WIDTH + LANES] = y


def _inproj(x, mod, w_in_bf16, tables):
    b, s, _ = x.shape
    tm = ROW_TILE
    table_spec = pl.BlockSpec((tm, LANES), lambda bi, si: (si, 0))
    return pl.pallas_call(
        _inproj_kernel,
        grid=(b, s // tm),
        in_specs=[
            pl.BlockSpec((1, tm, D_MODEL), lambda bi, si: (bi, si, 0)),
            pl.BlockSpec((1, 6, D_MODEL), lambda bi, si: (bi, 0, 0)),
            _resident((D_MODEL, IN_WIDTH), lambda bi, si: (0, 0)),
            table_spec, table_spec, table_spec, table_spec,
        ],
        out_specs=[
            pl.BlockSpec((1, tm, AB_WIDTH), lambda bi, si: (bi, si, 0)),
            pl.BlockSpec((1, tm, C_WIDTH), lambda bi, si: (bi, si, 0)),
        ],
        out_shape=[
            jax.ShapeDtypeStruct((b, s, AB_WIDTH), BF16),
            jax.ShapeDtypeStruct((b, s, C_WIDTH), F32),
        ],
        compiler_params=_params(("arbitrary", "arbitrary")),
        name="inproj",
    )(x, mod, w_in_bf16, *tables)


def _na_bias_table(rpb):
    c_idx = jnp.arange(GRID_W)
    c_start = jnp.clip(c_idx - NA_WIN_COLS // 2, 0, GRID_W - NA_WIN_COLS)
    col_in = (c_idx[None, :] >= c_start[:, None]) & (c_idx[None, :] < c_start[:, None] + NA_WIN_COLS)
    dc = jnp.clip(c_idx[None, :] - c_idx[:, None] + (NA_WIN_COLS - 1), 0, 2 * NA_WIN_COLS - 2)
    v_idx = jnp.arange(NA_WIN_ROWS)
    j_idx = jnp.arange(NA_WIN_ROWS)
    dr = j_idx[None, :] - v_idx[:, None] + (NA_WIN_ROWS - 1)
    bias = rpb.astype(F32)[:, dr][:, :, :, dc]
    bias = jnp.where(col_in[None, None, None], bias, NEG_INF)
    bias = bias.transpose(0, 1, 3, 2, 4)
    bias = bias.reshape(NA_HEADS // 2, 2, NA_WIN_ROWS, GRID_W, NA_WIN_ROWS * GRID_W)
    return bias.transpose(0, 2, 1, 3, 4).reshape(NA_HEADS // 2, NA_WIN_ROWS, 2 * GRID_W, NA_WIN_ROWS * GRID_W)


def _na_kernel(q_ref, k_ref, v_ref, bias_ref, o_ref, *, n_rows):
    step = pl.program_id(2)
    lane = lax.broadcasted_iota(jnp.int32, (GRID_W, LANES), 1)
    first = lane < HEAD_DIM
    win = NA_WIN_ROWS * GRID_W
    scores, vws = [], []
    for rr in range(NA_ROWS_PER_STEP):
        r = step * NA_ROWS_PER_STEP + rr
        row0 = jnp.clip(r - NA_WIN_ROWS // 2, 0, n_rows - NA_WIN_ROWS)
        start = pl.multiple_of(row0 * GRID_W, GRID_W)
        kw = k_ref[0, pl.ds(start, win), :]
        vws.append(v_ref[0, pl.ds(start, win), :])
        q = q_ref[0, rr * GRID_W:(rr + 1) * GRID_W, :]
        zero = jnp.zeros_like(q)
        q2 = jnp.concatenate([jnp.where(first, q, zero), jnp.where(first, zero, q)], axis=0)
        scores.append(lax.dot_general(q2, kw, _NT_DIMS, preferred_element_type=F32) + bias_ref[0, r - row0])
    probs, dens = [], []
    for s in scores:
        p = jnp.exp(s - jnp.max(s, axis=1, keepdims=True))
        dens.append(jnp.sum(p, axis=1, keepdims=True))
        probs.append(p.astype(BF16))
    for rr, (p, l, vw) in enumerate(zip(probs, dens, vws)):
        o2 = jnp.dot(p, vw, preferred_element_type=F32) / l
        o_ref[0, rr * GRID_W:(rr + 1) * GRID_W, :] = jnp.where(first, o2[:GRID_W], o2[GRID_W:]).astype(o_ref.dtype)


def _na(ab, bias_tbl):
    b, s, _ = ab.shape
    n_rows = s // GRID_W
    tq = NA_ROWS_PER_STEP * GRID_W
    k_blk = NA_WIDTH // LANES
    return pl.pallas_call(
        functools.partial(_na_kernel, n_rows=n_rows),
        grid=(b, NA_HEADS // 2, s // tq),
        in_specs=[
            pl.BlockSpec((1, tq, LANES), lambda bi, hp, i: (bi, i, hp)),
            pl.BlockSpec((1, s, LANES), lambda bi, hp, i: (bi, 0, k_blk + hp)),
            pl.BlockSpec((1, s, LANES), lambda bi, hp, i: (bi, 0, 2 * k_blk + hp)),
            pl.BlockSpec((1, NA_WIN_ROWS, 2 * GRID_W, NA_WIN_ROWS * GRID_W), lambda bi, hp, i: (hp, 0, 0, 0)),
        ],
        out_specs=pl.BlockSpec((1, tq, LANES), lambda bi, hp, i: (bi, i, hp)),
        out_shape=jax.ShapeDtypeStruct((b, s, NA_WIDTH), BF16),
        compiler_params=_params(("arbitrary", "arbitrary", "arbitrary")),
        name="na",
    )(ab, ab, ab, bias_tbl)


def _diff_kernel(q_ref, k_ref, v_ref, lam_ref, g_ref, o_ref, vt_scr, acc_scr, s_even, s_odd, qm_scr, *,
                 seq, lambda_init):
    n_kv = seq // DIFF_TK
    n_q = seq // DIFF_TQ
    assert n_kv % 2 == 0, "the last kv tile must read s_odd so that s_even is free for the next query tile"
    q_tile = pl.program_id(2)
    feat = lax.broadcasted_iota(jnp.int32, (LANES, DIFF_TQ), 0)
    group = lax.shift_right_logical(feat, DIFF_QK_DIM.bit_length() - 1)

    def masked_qt(tile_idx):
        start = pl.multiple_of(tile_idx * DIFF_TQ, DIFF_TQ)
        qt = q_ref[0, pl.ds(start, DIFF_TQ), :].astype(F32).T
        return [jnp.where(group == c, qt, 0.0).astype(BF16) for c in range(4)]

    def scores(t, c, qm, dst):
        start = pl.multiple_of(t * DIFF_TK, DIFF_TK)
        dst[c] = jnp.dot(k_ref[0, pl.ds(start, DIFF_TK), :], qm, preferred_element_type=F32)

    @pl.when(q_tile == 0)
    def _():
        ones = jnp.ones((DIFF_ACC_ROWS - HEAD_DIM, DIFF_TK), BF16)
        for t in range(n_kv):
            vt = v_ref[0, t * DIFF_TK:(t + 1) * DIFF_TK, :].astype(F32).T.astype(BF16)
            for h in range(2):
                vt_scr[t, h, 0:HEAD_DIM, :] = vt[h * HEAD_DIM:(h + 1) * HEAD_DIM]
                vt_scr[t, h, HEAD_DIM:DIFF_ACC_ROWS, :] = ones
        for c, qm in enumerate(masked_qt(0)):
            qm_scr[c] = qm
            scores(0, c, qm, s_even)

    qmts = [qm_scr[c] for c in range(4)]
    acc_scr[...] = jnp.zeros(acc_scr.shape, F32)

    def consume(t, ms, cur, nxt):
        last = t + 1 == n_kv
        if last:
            next_qms = masked_qt(jnp.minimum(q_tile + 1, n_q - 1))
        new_ms = []
        for c in range(4):
            if last:
                qm_scr[c] = next_qms[c]
                scores(0, c, next_qms[c], nxt)
            else:
                scores(t + 1, c, qmts[c], nxt)
            st = cur[c]
            m_new = jnp.maximum(ms[c], jnp.max(st, axis=0, keepdims=True))
            p = jnp.exp2(st - m_new).astype(BF16)
            alpha = jnp.exp2(ms[c] - m_new)
            acc_scr[c] = alpha * acc_scr[c] + jnp.dot(vt_scr[t, c // 2], p, preferred_element_type=F32)
            new_ms.append(m_new)
        return tuple(new_ms)

    ms = tuple(jnp.full((1, DIFF_TQ), NEG_INF, F32) for _ in range(4))
    for t in range(n_kv):
        cur, nxt = (s_even, s_odd) if t % 2 == 0 else (s_odd, s_even)
        ms = consume(t, ms, cur, nxt)

    lf = lam_ref[...]
    lam = (jnp.exp(jnp.sum(lf[0:1] * lf[1:2], axis=1, keepdims=True))
           - jnp.exp(jnp.sum(lf[2:3] * lf[3:4], axis=1, keepdims=True)) + lambda_init)
    o = [acc_scr[c, 0:HEAD_DIM, :] / acc_scr[c, HEAD_DIM:HEAD_DIM + 1, :] for c in range(4)]
    normed = []
    for h in range(2):
        d = o[2 * h] - lam * o[2 * h + 1]
        ms_h = jnp.sum(d * d, axis=0, keepdims=True) * (1.0 / HEAD_DIM)
        normed.append(d * lax.rsqrt(ms_h + LN_EPS))
    out = jnp.concatenate(normed, axis=0).T
    o_ref[0] = (out * g_ref[...] * (1.0 - lambda_init)).astype(o_ref.dtype)


def _diff(ab, lam_vecs, subln_g, lambda_init):
    b, s, _ = ab.shape
    q_blk = 3 * NA_WIDTH // LANES
    k_blk = q_blk + DIFF_WIDTH // LANES
    v_blk = k_blk + DIFF_WIDTH // LANES
    g_tile = jnp.tile(subln_g.astype(F32), LANES // HEAD_DIM).reshape(1, LANES)
    return pl.pallas_call(
        functools.partial(_diff_kernel, seq=s, lambda_init=lambda_init),
        grid=(b, DIFF_HEADS // 2, s // DIFF_TQ),
        in_specs=[
            pl.BlockSpec((1, s, LANES), lambda bi, hp, i: (bi, 0, q_blk + hp)),
            pl.BlockSpec((1, s, LANES), lambda bi, hp, i: (bi, 0, k_blk + hp)),
            pl.BlockSpec((1, s, LANES), lambda bi, hp, i: (bi, 0, v_blk + hp)),
            pl.BlockSpec((4, DIFF_QK_DIM), lambda bi, hp, i: (0, 0)),
            pl.BlockSpec((1, LANES), lambda bi, hp, i: (0, 0)),
        ],
        out_specs=pl.BlockSpec((1, DIFF_TQ, LANES), lambda bi, hp, i: (bi, i, hp)),
        out_shape=jax.ShapeDtypeStruct((b, s, DIFF_WIDTH), BF16),
        scratch_shapes=[
            pltpu.VMEM((s // DIFF_TK, 2, DIFF_ACC_ROWS, DIFF_TK), BF16),
            pltpu.VMEM((4, DIFF_ACC_ROWS, DIFF_TQ), F32),
            pltpu.VMEM((4, DIFF_TK, DIFF_TQ), F32),
            pltpu.VMEM((4, DIFF_TK, DIFF_TQ), F32),
            pltpu.VMEM((4, LANES, DIFF_TQ), BF16),
        ],
        compiler_params=_params(("arbitrary", "arbitrary", "arbitrary")),
        name="diff",
    )(ab, ab, ab, lam_vecs.astype(F32), g_tile)


def _dil_kernel(q_ref, k_ref, v_ref, o_ref, o_scr, m_scr, l_scr, bias_scr, *, seq):
    tile = pl.program_id(2)
    hw = DIL_HALF_WIDTH
    lane = lax.broadcasted_iota(jnp.int32, (hw, LANES), 1)
    first = lane < HEAD_DIM
    row = lax.broadcasted_iota(jnp.int32, (2 * hw, DIL_KEY_LANES), 0) & (hw - 1)
    col = lax.broadcasted_iota(jnp.int32, (2 * hw, DIL_KEY_LANES), 1)
    pad = jnp.zeros((DIL_KEY_LANES - 3 * hw, LANES), BF16)
    delta = col - hw - row
    band = jnp.minimum(delta + hw, hw - delta)
    for variant in range(4):
        col_min = hw if variant & 1 else 0
        col_max = (2 * hw if variant & 2 else 3 * hw) - 1
        ok = jnp.minimum(band, jnp.minimum(col - col_min, col_max - col)) >= 0
        bias_scr[variant] = jnp.where(ok, 0.0, NEG_INF)

    for pat, (_, dil) in enumerate(DIL_PATTERNS):
        blocks_per_tile = DIL_TILE // (hw * dil)
        n_blocks = seq // (hw * dil)
        units = [(u % dil, u // dil) for u in range(DIL_TILE // hw)]
        for g0 in range(0, len(units), DIL_GROUP):
            group = units[g0:g0 + DIL_GROUP]
            scores, vcats = [], []
            for res, lb in group:
                blk = tile * blocks_per_tile + lb
                q = q_ref[0, pl.ds(lb * (hw * dil) + res, hw, stride=dil), :].astype(BF16)
                zero = jnp.zeros_like(q)
                q2 = jnp.concatenate([jnp.where(first, q, zero), jnp.where(first, zero, q)], axis=0)
                ks, vs = [], []
                for j in (-1, 0, 1):
                    kb = jnp.clip(blk + j, 0, n_blocks - 1)
                    k_start = kb * (hw * dil) + res
                    ks.append(k_ref[0, pl.ds(k_start, hw, stride=dil), :])
                    vs.append(v_ref[0, pl.ds(k_start, hw, stride=dil), :])
                kcat = jnp.concatenate([jnp.concatenate(ks, axis=0).astype(BF16), pad], axis=0)
                vcats.append(jnp.concatenate([jnp.concatenate(vs, axis=0).astype(BF16), pad], axis=0))
                s = lax.dot_general(q2, kcat, _NT_DIMS, preferred_element_type=F32)
                variant = jnp.where(blk > 0, 0, 1) + jnp.where(blk < n_blocks - 1, 0, 2)
                scores.append(s + bias_scr[variant])
            probs, stats = [], []
            for s in scores:
                m = jnp.max(s, axis=1, keepdims=True)
                p = jnp.exp2(s - m)
                stats.append((m, jnp.sum(p, axis=1, keepdims=True)))
                probs.append(p.astype(BF16))
            for (res, lb), p, vcat, (m, l) in zip(group, probs, vcats, stats):
                o2 = jnp.dot(p, vcat, preferred_element_type=F32)
                rows = pl.ds(lb * (hw * dil) + res, hw, stride=dil)
                o_scr[pat, rows, :] = jnp.where(first, o2[:hw], o2[hw:])
                m_scr[pat, rows, :] = jnp.where(first, m[:hw], m[hw:])
                l_scr[pat, rows, :] = jnp.where(first, l[:hw], l[hw:])

    m_all = jnp.maximum(jnp.maximum(m_scr[0], m_scr[1]), m_scr[2])
    num = jnp.zeros((DIL_TILE, LANES), F32)
    den = jnp.zeros((DIL_TILE, LANES), F32)
    for pat in range(len(DIL_PATTERNS)):
        w = jnp.exp2(m_scr[pat] - m_all)
        num = num + w * o_scr[pat]
        den = den + w * l_scr[pat]
    o_ref[0] = (num / den).astype(o_ref.dtype)


def _dil(c):
    b, s, _ = c.shape
    k_blk = DIL_WIDTH // LANES
    n_pat = len(DIL_PATTERNS)
    return pl.pallas_call(
        functools.partial(_dil_kernel, seq=s),
        grid=(b, DIL_HEADS // 2, s // DIL_TILE),
        in_specs=[
            pl.BlockSpec((1, DIL_TILE, LANES), lambda bi, hp, i: (bi, i, hp)),
            pl.BlockSpec((1, s, LANES), lambda bi, hp, i: (bi, 0, k_blk + hp)),
            pl.BlockSpec((1, s, LANES), lambda bi, hp, i: (bi, 0, 2 * k_blk + hp)),
        ],
        out_specs=pl.BlockSpec((1, DIL_TILE, LANES), lambda bi, hp, i: (bi, i, hp)),
        out_shape=jax.ShapeDtypeStruct((b, s, DIL_WIDTH), BF16),
        scratch_shapes=[
            pltpu.VMEM((n_pat, DIL_TILE, LANES), F32),
            pltpu.VMEM((n_pat, DIL_TILE, LANES), F32),
            pltpu.VMEM((n_pat, DIL_TILE, LANES), F32),
            pltpu.VMEM((4, 2 * DIL_HALF_WIDTH, DIL_KEY_LANES), F32),
        ],
        compiler_params=_params(("arbitrary", "arbitrary", "arbitrary")),
        name="dil",
    )(c, c, c)


def _layer_norm(z, g, b):
    mu = jnp.mean(z, axis=1, keepdims=True)
    zc = z - mu
    var = jnp.mean(zc * zc, axis=1, keepdims=True)
    return zc * lax.rsqrt(var + LN_EPS) * g + b


def _outproj_kernel(oa_ref, ob_ref, oc_ref, x_ref, mod_ref, w_ref, g_ref, b_ref, y_ref):
    g1 = mod_ref[0, 2:3, :]
    for sub in range(OUTPROJ_SUBTILES):
        rows = slice(sub * ROW_TILE, (sub + 1) * ROW_TILE)
        mix = jnp.dot(oa_ref[0, rows, :], w_ref[0:NA_WIDTH, :], preferred_element_type=F32)
        mix = mix + jnp.dot(ob_ref[0, rows, :], w_ref[NA_WIDTH:NA_WIDTH + DIFF_WIDTH, :],
                            preferred_element_type=F32)
        mix = mix + jnp.dot(oc_ref[0, rows, :], w_ref[NA_WIDTH + DIFF_WIDTH:MIX_WIDTH, :],
                            preferred_element_type=F32)
        z = DEEPNORM_ALPHA * x_ref[0, rows, :] + g1 * mix
        y_ref[0, rows, :] = _layer_norm(z, g_ref[...], b_ref[...])


def _outproj(oa, ob, oc, x, mod, w_out_bf16, ln_g, ln_b):
    b, s, _ = x.shape
    tm = OUTPROJ_SUBTILES * ROW_TILE
    row = lambda width: pl.BlockSpec((1, tm, width), lambda bi, si: (bi, si, 0))
    vec = pl.BlockSpec((1, D_MODEL), lambda bi, si: (0, 0))
    return pl.pallas_call(
        _outproj_kernel,
        grid=(b, s // tm),
        in_specs=[
            row(NA_WIDTH), row(DIFF_WIDTH), row(DIL_WIDTH), row(D_MODEL),
            pl.BlockSpec((1, 6, D_MODEL), lambda bi, si: (bi, 0, 0)),
            _resident((MIX_WIDTH, D_MODEL), lambda bi, si: (0, 0)),
            vec, vec,
        ],
        out_specs=row(D_MODEL),
        out_shape=jax.ShapeDtypeStruct((b, s, D_MODEL), F32),
        compiler_params=_params(("arbitrary", "arbitrary")),
        name="outproj",
    )(oa, ob, oc, x, mod, w_out_bf16, ln_g.reshape(1, D_MODEL), ln_b.reshape(1, D_MODEL))


def _ffn_kernel(x_ref, mod_ref, wgu_ref, wd_ref, g_ref, b_ref, y_ref):
    sh2 = mod_ref[0, 3:4, :]
    sc2 = mod_ref[0, 4:5, :]
    g2 = mod_ref[0, 5:6, :]
    x = x_ref[0]
    h = (x * (1.0 + sc2) + sh2).astype(BF16)
    acc = jnp.zeros((x.shape[0], D_MODEL), F32)
    for c0, c1 in FFN_CHUNKS:
        gate = jnp.dot(h, wgu_ref[:, c0:c1], preferred_element_type=F32)
        up = jnp.dot(h, wgu_ref[:, FFN_HIDDEN + c0:FFN_HIDDEN + c1], preferred_element_type=F32)
        act = (gate / (1.0 + jnp.exp(-gate)) * up).astype(BF16)
        acc = acc + jnp.dot(act, wd_ref[c0:c1, :], preferred_element_type=F32)
    z = DEEPNORM_ALPHA * x + g2 * acc
    y_ref[0] = _layer_norm(z, g_ref[...], b_ref[...])


def _ffn(x, mod, w_gu_bf16, w_down_bf16, ln_g, ln_b):
    b, s, _ = x.shape
    tm = ROW_TILE
    row = pl.BlockSpec((1, tm, D_MODEL), lambda bi, si: (bi, si, 0))
    vec = pl.BlockSpec((1, D_MODEL), lambda bi, si: (0, 0))
    return pl.pallas_call(
        _ffn_kernel,
        grid=(b, s // tm),
        in_specs=[
            row,
            pl.BlockSpec((1, 6, D_MODEL), lambda bi, si: (bi, 0, 0)),
            _resident((D_MODEL, 2 * FFN_HIDDEN), lambda bi, si: (0, 0)),
            _resident((FFN_HIDDEN, D_MODEL), lambda bi, si: (0, 0)),
            vec, vec,
        ],
        out_specs=row,
        out_shape=jax.ShapeDtypeStruct((b, s, D_MODEL), F32),
        compiler_params=_params(("arbitrary", "arbitrary")),
        name="ffn",
    )(x, mod, w_gu_bf16, w_down_bf16, ln_g.reshape(1, D_MODEL), ln_b.reshape(1, D_MODEL))


def _layer(x, mod, layer, tables, w_in_b, bias_tbl, diff_lambda, diff_subln_g, w_out_b,
           ln1_g, ln1_b, w_gu_b, w_down_b, ln2_g, ln2_b):
    lambda_init = 0.8 - 0.6 * math.exp(-0.3 * layer)
    ab, c = _inproj(x, mod, w_in_b, tables)
    oa = _na(ab, bias_tbl)
    ob = _diff(ab, diff_lambda, diff_subln_g, lambda_init)
    oc = _dil(c)
    x = _outproj(oa, ob, oc, x, mod, w_out_b, ln1_g, ln1_b)
    return _ffn(x, mod, w_gu_b, w_down_b, ln2_g, ln2_b)


def kernel(x_prompt, x_sample, c_prompt, c_sample, w_ada, b_ada, w_in, na_rpb, diff_lambda, diff_subln_g,
           w_out, ln1_g, ln1_b, w_gu, w_down, ln2_g, ln2_b):
    xs = [x_prompt, x_sample]
    n_prompt = c_prompt.shape[0]
    mod_all = _ada(jnp.concatenate([c_prompt, c_sample], axis=0), w_ada, b_ada)
    tables = [_rope_tables(x.shape[1], DIFF_QK_DIM) + _rope_tables(x.shape[1], HEAD_DIM) for x in xs]
    for layer in range(DEPTH):
        w_in_b = w_in[layer].astype(BF16)
        w_out_b = w_out[layer].astype(BF16)
        w_gu_b = w_gu[layer].astype(BF16)
        w_down_b = w_down[layer].astype(BF16)
        bias_tbl = _na_bias_table(na_rpb[layer])
        for g in range(2):
            nb = xs[g].shape[0]
            lo = 0 if g == 0 else n_prompt
            mod = mod_all[layer, lo:lo + nb].reshape(nb, 6, D_MODEL)
            xs[g] = _layer(xs[g], mod, layer, tables[g], w_in_b, bias_tbl, diff_lambda[layer],
                           diff_subln_g[layer], w_out_b, ln1_g[layer], ln1_b[layer], w_gu_b, w_down_b,
                           ln2_g[layer], ln2_b[layer])
    return tuple(xs)
```

```python
import functools
import math

import jax
import jax.numpy as jnp
from jax import lax
from jax.experimental import pallas as pl
from jax.experimental.pallas import tpu as pltpu

F32 = jnp.float32
BF16 = jnp.bfloat16

D_MODEL = 1024
DEPTH = 2
HEAD_DIM = 64
NA_HEADS = 4
DIFF_HEADS = 4
DIL_HEADS = 8
NA_WIDTH = NA_HEADS * HEAD_DIM
DIFF_WIDTH = DIFF_HEADS * HEAD_DIM
DIL_WIDTH = DIL_HEADS * HEAD_DIM
MIX_WIDTH = NA_WIDTH + DIFF_WIDTH + DIL_WIDTH
IN_WIDTH = 3 * MIX_WIDTH
AB_WIDTH = 3 * NA_WIDTH + 3 * DIFF_WIDTH
C_WIDTH = 3 * DIL_WIDTH
GRID_W = 64
NA_WIN_ROWS = 8
NA_WIN_COLS = 16
DIFF_QK_DIM = HEAD_DIM // 2
DIL_PATTERNS = ((128, 1), (512, 4), (2048, 16))
DIL_HALF_WIDTH = 64
FFN_HIDDEN = 2816
ROPE_THETA = 10000.0
LN_EPS = 1e-5
DEEPNORM_ALPHA = (2 * DEPTH) ** 0.25
NEG_INF = -1e30
LOG2E = math.log2(math.e)

LANES = 128
MXU_WIDTH = 256
VMEM_LIMIT_BYTES = 48 * 1024 * 1024

ROW_TILE = 512
INPROJ_ROW_TILE = 2 * ROW_TILE
OUTPROJ_SUBTILES = 2
FFN_SUBTILES = 2
NA_ROWS_PER_STEP = 32
DIFF_TQ = 256
DIFF_TK = 512
DIFF_KV_TILES_PER_STEP = 1
DIFF_ACC_ROWS = HEAD_DIM + 16
DIL_TILE = 2048
DIL_KEY_LANES = 2 * LANES
DIL_GROUP = 16
FFN_CHUNKS = ((0, 1024), (1024, 2048), (2048, FFN_HIDDEN))

_NT_DIMS = (((1,), (1,)), ((), ()))


def _params(semantics):
    return pltpu.CompilerParams(dimension_semantics=semantics, vmem_limit_bytes=VMEM_LIMIT_BYTES)


def _resident(block_shape, index_map):
    return pl.BlockSpec(block_shape, index_map, pipeline_mode=pl.Buffered(1))


def _ada_kernel(c_ref, w_ref, b_ref, o_ref):
    c = c_ref[...]
    a = c / (1.0 + jnp.exp(-c))
    o_ref[0] = jnp.dot(a, w_ref[0], precision=lax.Precision.HIGHEST,
                       preferred_element_type=F32) + b_ref[0]


def _ada(c_all, w_ada, b_ada):
    nb = c_all.shape[0]
    n_out = w_ada.shape[-1]
    tn = 1536
    return pl.pallas_call(
        _ada_kernel,
        grid=(DEPTH, n_out // tn),
        in_specs=[
            pl.BlockSpec((nb, D_MODEL), lambda l, n: (0, 0)),
            pl.BlockSpec((1, D_MODEL, tn), lambda l, n: (l, 0, n)),
            pl.BlockSpec((1, 1, tn), lambda l, n: (l, 0, n)),
        ],
        out_specs=pl.BlockSpec((1, nb, tn), lambda l, n: (l, 0, n)),
        out_shape=jax.ShapeDtypeStruct((DEPTH, nb, n_out), F32),
        compiler_params=_params(("arbitrary", "arbitrary")),
        name="ada",
    )(c_all, w_ada, b_ada.reshape(DEPTH, 1, n_out))


def _rope_tables(seq, dim):
    half = dim // 2
    inv_freq = ROPE_THETA ** (-jnp.arange(half, dtype=F32) / half)
    ang = jnp.arange(seq, dtype=F32)[:, None] * inv_freq[None, :]
    cos = jnp.cos(ang)
    sin = jnp.sin(ang)
    reps = LANES // dim
    cos_t = jnp.tile(jnp.concatenate([cos, cos], axis=1), (1, reps))
    sin_t = jnp.tile(jnp.concatenate([-sin, sin], axis=1), (1, reps))
    return cos_t, sin_t


def _rope(y, cos, sin_signed, half):
    lane = lax.broadcasted_iota(jnp.int32, y.shape, 1)
    first = (lane & (2 * half - 1)) < half
    partner = jnp.where(first, pltpu.roll(y, LANES - half, axis=1), pltpu.roll(y, half, axis=1))
    return y * cos + partner * sin_signed


_COLUMN_CLASSES = (
    (0, NA_WIDTH, None, HEAD_DIM ** -0.5),
    (NA_WIDTH, 3 * NA_WIDTH, None, None),
    (3 * NA_WIDTH, 3 * NA_WIDTH + DIFF_WIDTH, DIFF_QK_DIM, DIFF_QK_DIM ** -0.5 * LOG2E),
    (3 * NA_WIDTH + DIFF_WIDTH, 3 * NA_WIDTH + 2 * DIFF_WIDTH, DIFF_QK_DIM, None),
    (3 * NA_WIDTH + 2 * DIFF_WIDTH, AB_WIDTH, None, None),
    (AB_WIDTH, AB_WIDTH + DIL_WIDTH, HEAD_DIM, HEAD_DIM ** -0.5 * LOG2E),
    (AB_WIDTH + DIL_WIDTH, AB_WIDTH + 2 * DIL_WIDTH, HEAD_DIM, None),
    (AB_WIDTH + 2 * DIL_WIDTH, IN_WIDTH, None, None),
)


def _column_class(col):
    for lo, hi, rope_dim, scale in _COLUMN_CLASSES:
        if lo <= col < hi:
            return rope_dim, scale
    raise ValueError(col)


def _inproj_kernel(x_ref, mod_ref, w_ref, cos32_ref, sin32_ref, cos64_ref, sin64_ref, ab_ref, c_ref):
    sh1 = mod_ref[0, 0:1, :]
    sc1 = mod_ref[0, 1:2, :]
    h = (x_ref[0] * (1.0 + sc1) + sh1).astype(BF16)
    for n in range(IN_WIDTH // MXU_WIDTH):
        acc = jnp.dot(h, w_ref[:, n * MXU_WIDTH:(n + 1) * MXU_WIDTH], preferred_element_type=F32)
        for part in range(MXU_WIDTH // LANES):
            col = n * MXU_WIDTH + part * LANES
            y = acc[:, part * LANES:(part + 1) * LANES]
            rope_dim, scale = _column_class(col)
            if rope_dim == DIFF_QK_DIM:
                y = _rope(y, cos32_ref[...], sin32_ref[...], rope_dim // 2)
            elif rope_dim == HEAD_DIM:
                y = _rope(y, cos64_ref[...], sin64_ref[...], rope_dim // 2)
            if scale is not None:
                y = y * scale
            if col < AB_WIDTH:
                ab_ref[0, :, col:col + LANES] = y.astype(ab_ref.dtype)
            else:
                c_ref[0, :, col - AB_WIDTH:col - AB_WIDTH + LANES] = y


def _inproj(x, mod, w_in_bf16, tables):
    b, s, _ = x.shape
    tm = INPROJ_ROW_TILE
    table_spec = pl.BlockSpec((tm, LANES), lambda bi, si: (si, 0))
    return pl.pallas_call(
        _inproj_kernel,
        grid=(b, s // tm),
        in_specs=[
            pl.BlockSpec((1, tm, D_MODEL), lambda bi, si: (bi, si, 0)),
            pl.BlockSpec((1, 6, D_MODEL), lambda bi, si: (bi, 0, 0)),
            _resident((D_MODEL, IN_WIDTH), lambda bi, si: (0, 0)),
            table_spec, table_spec, table_spec, table_spec,
        ],
        out_specs=[
            pl.BlockSpec((1, tm, AB_WIDTH), lambda bi, si: (bi, si, 0)),
            pl.BlockSpec((1, tm, C_WIDTH), lambda bi, si: (bi, si, 0)),
        ],
        out_shape=[
            jax.ShapeDtypeStruct((b, s, AB_WIDTH), BF16),
            jax.ShapeDtypeStruct((b, s, C_WIDTH), F32),
        ],
        compiler_params=_params(("arbitrary", "arbitrary")),
        name="inproj",
    )(x, mod, w_in_bf16, *tables)


def _na_bias_table(rpb):
    c_idx = jnp.arange(GRID_W)
    c_start = jnp.clip(c_idx - NA_WIN_COLS // 2, 0, GRID_W - NA_WIN_COLS)
    col_in = (c_idx[None, :] >= c_start[:, None]) & (c_idx[None, :] < c_start[:, None] + NA_WIN_COLS)
    dc = jnp.clip(c_idx[None, :] - c_idx[:, None] + (NA_WIN_COLS - 1), 0, 2 * NA_WIN_COLS - 2)
    v_idx = jnp.arange(NA_WIN_ROWS)
    j_idx = jnp.arange(NA_WIN_ROWS)
    dr = j_idx[None, :] - v_idx[:, None] + (NA_WIN_ROWS - 1)
    bias = rpb.astype(F32)[:, dr][:, :, :, dc]
    bias = jnp.where(col_in[None, None, None], bias, NEG_INF)
    bias = bias.transpose(0, 1, 3, 2, 4)
    bias = bias.reshape(NA_HEADS // 2, 2, NA_WIN_ROWS, GRID_W, NA_WIN_ROWS * GRID_W)
    return bias.transpose(0, 2, 1, 3, 4).reshape(NA_HEADS // 2, NA_WIN_ROWS, 2 * GRID_W, NA_WIN_ROWS * GRID_W)


def _na_kernel(q_ref, k_ref, v_ref, bias_ref, o_ref, *, n_rows):
    step = pl.program_id(2)
    lane = lax.broadcasted_iota(jnp.int32, (GRID_W, LANES), 1)
    first = lane < HEAD_DIM
    win = NA_WIN_ROWS * GRID_W
    scores, vws = [], []
    for rr in range(NA_ROWS_PER_STEP):
        r = step * NA_ROWS_PER_STEP + rr
        row0 = jnp.clip(r - NA_WIN_ROWS // 2, 0, n_rows - NA_WIN_ROWS)
        start = pl.multiple_of(row0 * GRID_W, GRID_W)
        kw = k_ref[0, pl.ds(start, win), :]
        vws.append(v_ref[0, pl.ds(start, win), :])
        q = q_ref[0, rr * GRID_W:(rr + 1) * GRID_W, :]
        zero = jnp.zeros_like(q)
        q2 = jnp.concatenate([jnp.where(first, q, zero), jnp.where(first, zero, q)], axis=0)
        scores.append(lax.dot_general(q2, kw, _NT_DIMS, preferred_element_type=F32) + bias_ref[0, r - row0])
    probs, dens = [], []
    for s in scores:
        p = jnp.exp(s - jnp.max(s, axis=1, keepdims=True))
        dens.append(jnp.sum(p, axis=1, keepdims=True))
        probs.append(p.astype(BF16))
    for rr, (p, l, vw) in enumerate(zip(probs, dens, vws)):
        o2 = jnp.dot(p, vw, preferred_element_type=F32) / l
        o_ref[0, rr * GRID_W:(rr + 1) * GRID_W, :] = jnp.where(first, o2[:GRID_W], o2[GRID_W:]).astype(o_ref.dtype)


def _na(ab, bias_tbl):
    b, s, _ = ab.shape
    n_rows = s // GRID_W
    tq = NA_ROWS_PER_STEP * GRID_W
    k_blk = NA_WIDTH // LANES
    return pl.pallas_call(
        functools.partial(_na_kernel, n_rows=n_rows),
        grid=(b, NA_HEADS // 2, s // tq),
        in_specs=[
            pl.BlockSpec((1, tq, LANES), lambda bi, hp, i: (bi, i, hp)),
            pl.BlockSpec((1, s, LANES), lambda bi, hp, i: (bi, 0, k_blk + hp)),
            pl.BlockSpec((1, s, LANES), lambda bi, hp, i: (bi, 0, 2 * k_blk + hp)),
            pl.BlockSpec((1, NA_WIN_ROWS, 2 * GRID_W, NA_WIN_ROWS * GRID_W), lambda bi, hp, i: (hp, 0, 0, 0)),
        ],
        out_specs=pl.BlockSpec((1, tq, LANES), lambda bi, hp, i: (bi, i, hp)),
        out_shape=jax.ShapeDtypeStruct((b, s, NA_WIDTH), BF16),
        compiler_params=_params(("arbitrary", "arbitrary", "arbitrary")),
        name="na",
    )(ab, ab, ab, bias_tbl)


def _diff_kernel(q_ref, k_ref, v_ref, lam_ref, g_ref, o_ref, vt_scr, acc_scr, s_even, s_odd, qm_scr, *,
                 seq, lambda_init):
    n_kv = seq // DIFF_TK
    n_q = seq // DIFF_TQ
    n_sub = _diff_query_tiles_per_step(seq)
    assert n_kv % 2 == 0, "the last kv tile must read s_odd so that s_even is free for the next query tile"
    step = pl.program_id(2)
    feat = lax.broadcasted_iota(jnp.int32, (LANES, DIFF_TQ), 0)
    group = lax.shift_right_logical(feat, DIFF_QK_DIM.bit_length() - 1)

    def masked_qt(tile_idx):
        start = pl.multiple_of(tile_idx * DIFF_TQ, DIFF_TQ)
        qt = q_ref[0, pl.ds(start, DIFF_TQ), :].astype(F32).T
        return [jnp.where(group == c, qt, 0.0).astype(BF16) for c in range(4)]

    def scores(t, c, qm, dst):
        start = pl.multiple_of(t * DIFF_TK, DIFF_TK)
        dst[c] = jnp.dot(k_ref[0, pl.ds(start, DIFF_TK), :], qm, preferred_element_type=F32)

    @pl.when(step == 0)
    def _():
        ones = jnp.ones((DIFF_ACC_ROWS - HEAD_DIM, DIFF_TK), BF16)
        for t in range(n_kv):
            vt = v_ref[0, t * DIFF_TK:(t + 1) * DIFF_TK, :].astype(F32).T.astype(BF16)
            for h in range(2):
                vt_scr[t, h, 0:HEAD_DIM, :] = vt[h * HEAD_DIM:(h + 1) * HEAD_DIM]
                vt_scr[t, h, HEAD_DIM:DIFF_ACC_ROWS, :] = ones
        for c, qm in enumerate(masked_qt(0)):
            qm_scr[c] = qm
            scores(0, c, qm, s_even)

    lf = lam_ref[...]
    lam = (jnp.exp(jnp.sum(lf[0:1] * lf[1:2], axis=1, keepdims=True))
           - jnp.exp(jnp.sum(lf[2:3] * lf[3:4], axis=1, keepdims=True)) + lambda_init)

    for sub in range(n_sub):
        q_tile = step * n_sub + sub
        qmts = [qm_scr[c] for c in range(4)]
        acc_scr[...] = jnp.zeros(acc_scr.shape, F32)
        ms = tuple(jnp.full((1, DIFF_TQ), NEG_INF, F32) for _ in range(4))
        for t in range(n_kv):
            cur, nxt = (s_even, s_odd) if t % 2 == 0 else (s_odd, s_even)
            last = t + 1 == n_kv
            if last:
                next_qms = masked_qt(jnp.minimum(q_tile + 1, n_q - 1))
            new_ms = []
            for c in range(4):
                if last:
                    qm_scr[c] = next_qms[c]
                    scores(0, c, next_qms[c], nxt)
                else:
                    scores(t + 1, c, qmts[c], nxt)
                st = cur[c]
                m_new = jnp.maximum(ms[c], jnp.max(st, axis=0, keepdims=True))
                p = jnp.exp2(st - m_new).astype(BF16)
                alpha = jnp.exp2(ms[c] - m_new)
                acc_scr[c] = alpha * acc_scr[c] + jnp.dot(vt_scr[t, c // 2], p, preferred_element_type=F32)
                new_ms.append(m_new)
            ms = tuple(new_ms)

        o = [acc_scr[c, 0:HEAD_DIM, :] / acc_scr[c, HEAD_DIM:HEAD_DIM + 1, :] for c in range(4)]
        normed = []
        for h in range(2):
            d = o[2 * h] - lam * o[2 * h + 1]
            ms_h = jnp.sum(d * d, axis=0, keepdims=True) * (1.0 / HEAD_DIM)
            normed.append(d * lax.rsqrt(ms_h + LN_EPS))
        out = jnp.concatenate(normed, axis=0).T
        o_ref[0, sub * DIFF_TQ:(sub + 1) * DIFF_TQ, :] = (out * g_ref[...] * (1.0 - lambda_init)).astype(o_ref.dtype)


def _diff_query_tiles_per_step(seq):
    n_sub = max(1, DIFF_KV_TILES_PER_STEP // (seq // DIFF_TK))
    assert (seq // DIFF_TQ) % n_sub == 0
    return n_sub


def _diff(ab, lam_vecs, subln_g, lambda_init):
    b, s, _ = ab.shape
    q_blk = 3 * NA_WIDTH // LANES
    k_blk = q_blk + DIFF_WIDTH // LANES
    v_blk = k_blk + DIFF_WIDTH // LANES
    g_tile = jnp.tile(subln_g.astype(F32), LANES // HEAD_DIM).reshape(1, LANES)
    tq_step = DIFF_TQ * _diff_query_tiles_per_step(s)
    return pl.pallas_call(
        functools.partial(_diff_kernel, seq=s, lambda_init=lambda_init),
        grid=(b, DIFF_HEADS // 2, s // tq_step),
        in_specs=[
            pl.BlockSpec((1, s, LANES), lambda bi, hp, i: (bi, 0, q_blk + hp)),
            pl.BlockSpec((1, s, LANES), lambda bi, hp, i: (bi, 0, k_blk + hp)),
            pl.BlockSpec((1, s, LANES), lambda bi, hp, i: (bi, 0, v_blk + hp)),
            pl.BlockSpec((4, DIFF_QK_DIM), lambda bi, hp, i: (0, 0)),
            pl.BlockSpec((1, LANES), lambda bi, hp, i: (0, 0)),
        ],
        out_specs=pl.BlockSpec((1, tq_step, LANES), lambda bi, hp, i: (bi, i, hp)),
        out_shape=jax.ShapeDtypeStruct((b, s, DIFF_WIDTH), BF16),
        scratch_shapes=[
            pltpu.VMEM((s // DIFF_TK, 2, DIFF_ACC_ROWS, DIFF_TK), BF16),
            pltpu.VMEM((4, DIFF_ACC_ROWS, DIFF_TQ), F32),
            pltpu.VMEM((4, DIFF_TK, DIFF_TQ), F32),
            pltpu.VMEM((4, DIFF_TK, DIFF_TQ), F32),
            pltpu.VMEM((4, LANES, DIFF_TQ), BF16),
        ],
        compiler_params=_params(("arbitrary", "arbitrary", "arbitrary")),
        name="diff",
    )(ab, ab, ab, lam_vecs.astype(F32), g_tile)


def _dil_kernel(q_ref, k_ref, v_ref, o_ref, o_scr, m_scr, l_scr, bias_scr, *, seq):
    tile = pl.program_id(2)
    hw = DIL_HALF_WIDTH
    lane = lax.broadcasted_iota(jnp.int32, (hw, LANES), 1)
    first = lane < HEAD_DIM
    row = lax.broadcasted_iota(jnp.int32, (2 * hw, DIL_KEY_LANES), 0) & (hw - 1)
    col = lax.broadcasted_iota(jnp.int32, (2 * hw, DIL_KEY_LANES), 1)
    pad = jnp.zeros((DIL_KEY_LANES - 3 * hw, LANES), BF16)
    delta = col - hw - row
    band = jnp.minimum(delta + hw, hw - delta)
    for variant in range(4):
        col_min = hw if variant & 1 else 0
        col_max = (2 * hw if variant & 2 else 3 * hw) - 1
        ok = jnp.minimum(band, jnp.minimum(col - col_min, col_max - col)) >= 0
        bias_scr[variant] = jnp.where(ok, 0.0, NEG_INF)

    for pat, (_, dil) in enumerate(DIL_PATTERNS):
        blocks_per_tile = DIL_TILE // (hw * dil)
        n_blocks = seq // (hw * dil)
        units = [(u % dil, u // dil) for u in range(DIL_TILE // hw)]
        for g0 in range(0, len(units), DIL_GROUP):
            group = units[g0:g0 + DIL_GROUP]
            scores, vcats = [], []
            for res, lb in group:
                blk = tile * blocks_per_tile + lb
                q = q_ref[0, pl.ds(lb * (hw * dil) + res, hw, stride=dil), :].astype(BF16)
                zero = jnp.zeros_like(q)
                q2 = jnp.concatenate([jnp.where(first, q, zero), jnp.where(first, zero, q)], axis=0)
                ks, vs = [], []
                for j in (-1, 0, 1):
                    kb = jnp.clip(blk + j, 0, n_blocks - 1)
                    k_start = kb * (hw * dil) + res
                    ks.append(k_ref[0, pl.ds(k_start, hw, stride=dil), :])
                    vs.append(v_ref[0, pl.ds(k_start, hw, stride=dil), :])
                kcat = jnp.concatenate([jnp.concatenate(ks, axis=0).astype(BF16), pad], axis=0)
                vcats.append(jnp.concatenate([jnp.concatenate(vs, axis=0).astype(BF16), pad], axis=0))
                s = lax.dot_general(q2, kcat, _NT_DIMS, preferred_element_type=F32)
                variant = jnp.where(blk > 0, 0, 1) + jnp.where(blk < n_blocks - 1, 0, 2)
                scores.append(s + bias_scr[variant])
            probs, stats = [], []
            for s in scores:
                m = jnp.max(s, axis=1, keepdims=True)
                p = jnp.exp2(s - m)
                stats.append((m, jnp.sum(p, axis=1, keepdims=True)))
                probs.append(p.astype(BF16))
            for (res, lb), p, vcat, (m, l) in zip(group, probs, vcats, stats):
                o2 = jnp.dot(p, vcat, preferred_element_type=F32)
                rows = pl.ds(lb * (hw * dil) + res, hw, stride=dil)
                o_scr[pat, rows, :] = jnp.where(first, o2[:hw], o2[hw:])
                m_scr[pat, rows, :] = jnp.where(first, m[:hw], m[hw:])
                l_scr[pat, rows, :] = jnp.where(first, l[:hw], l[hw:])

    m_all = jnp.maximum(jnp.maximum(m_scr[0], m_scr[1]), m_scr[2])
    num = jnp.zeros((DIL_TILE, LANES), F32)
    den = jnp.zeros((DIL_TILE, LANES), F32)
    for pat in range(len(DIL_PATTERNS)):
        w = jnp.exp2(m_scr[pat] - m_all)
        num = num + w * o_scr[pat]
        den = den + w * l_scr[pat]
    o_ref[0] = (num / den).astype(o_ref.dtype)


def _dil(c):
    b, s, _ = c.shape
    k_blk = DIL_WIDTH // LANES
    n_pat = len(DIL_PATTERNS)
    return pl.pallas_call(
        functools.partial(_dil_kernel, seq=s),
        grid=(b, DIL_HEADS // 2, s // DIL_TILE),
        in_specs=[
            pl.BlockSpec((1, DIL_TILE, LANES), lambda bi, hp, i: (bi, i, hp)),
            pl.BlockSpec((1, s, LANES), lambda bi, hp, i: (bi, 0, k_blk + hp)),
            pl.BlockSpec((1, s, LANES), lambda bi, hp, i: (bi, 0, 2 * k_blk + hp)),
        ],
        out_specs=pl.BlockSpec((1, DIL_TILE, LANES), lambda bi, hp, i: (bi, i, hp)),
        out_shape=jax.ShapeDtypeStruct((b, s, DIL_WIDTH), BF16),
        scratch_shapes=[
            pltpu.VMEM((n_pat, DIL_TILE, LANES), F32),
            pltpu.VMEM((n_pat, DIL_TILE, LANES), F32),
            pltpu.VMEM((n_pat, DIL_TILE, LANES), F32),
            pltpu.VMEM((4, 2 * DIL_HALF_WIDTH, DIL_KEY_LANES), F32),
        ],
        compiler_params=_params(("arbitrary", "arbitrary", "arbitrary")),
        name="dil",
    )(c, c, c)


def _layer_norm(z, g, b):
    mu = jnp.mean(z, axis=1, keepdims=True)
    zc = z - mu
    var = jnp.mean(zc * zc, axis=1, keepdims=True)
    return zc * lax.rsqrt(var + LN_EPS) * g + b


def _outproj_kernel(oa_ref, ob_ref, oc_ref, x_ref, mod_ref, w_ref, g_ref, b_ref, y_ref):
    g1 = mod_ref[0, 2:3, :]
    for sub in range(OUTPROJ_SUBTILES):
        rows = slice(sub * ROW_TILE, (sub + 1) * ROW_TILE)
        mix = jnp.dot(oa_ref[0, rows, :], w_ref[0:NA_WIDTH, :], preferred_element_type=F32)
        mix = mix + jnp.dot(ob_ref[0, rows, :], w_ref[NA_WIDTH:NA_WIDTH + DIFF_WIDTH, :],
                            preferred_element_type=F32)
        mix = mix + jnp.dot(oc_ref[0, rows, :], w_ref[NA_WIDTH + DIFF_WIDTH:MIX_WIDTH, :],
                            preferred_element_type=F32)
        z = DEEPNORM_ALPHA * x_ref[0, rows, :] + g1 * mix
        y_ref[0, rows, :] = _layer_norm(z, g_ref[...], b_ref[...])


def _outproj(oa, ob, oc, x, mod, w_out_bf16, ln_g, ln_b):
    b, s, _ = x.shape
    tm = OUTPROJ_SUBTILES * ROW_TILE
    row = lambda width: pl.BlockSpec((1, tm, width), lambda bi, si: (bi, si, 0))
    vec = pl.BlockSpec((1, D_MODEL), lambda bi, si: (0, 0))
    return pl.pallas_call(
        _outproj_kernel,
        grid=(b, s // tm),
        in_specs=[
            row(NA_WIDTH), row(DIFF_WIDTH), row(DIL_WIDTH), row(D_MODEL),
            pl.BlockSpec((1, 6, D_MODEL), lambda bi, si: (bi, 0, 0)),
            _resident((MIX_WIDTH, D_MODEL), lambda bi, si: (0, 0)),
            vec, vec,
        ],
        out_specs=row(D_MODEL),
        out_shape=jax.ShapeDtypeStruct((b, s, D_MODEL), F32),
        compiler_params=_params(("arbitrary", "arbitrary")),
        name="outproj",
    )(oa, ob, oc, x, mod, w_out_bf16, ln_g.reshape(1, D_MODEL), ln_b.reshape(1, D_MODEL))


def _ffn_kernel(x_ref, mod_ref, wgu_ref, wd_ref, g_ref, b_ref, y_ref):
    sh2 = mod_ref[0, 3:4, :]
    sc2 = mod_ref[0, 4:5, :]
    g2 = mod_ref[0, 5:6, :]
    for sub in range(FFN_SUBTILES):
        rows = slice(sub * ROW_TILE, (sub + 1) * ROW_TILE)
        x = x_ref[0, rows, :]
        h = (x * (1.0 + sc2) + sh2).astype(BF16)
        acc = jnp.zeros((ROW_TILE, D_MODEL), F32)
        for c0, c1 in FFN_CHUNKS:
            gate = jnp.dot(h, wgu_ref[:, c0:c1], preferred_element_type=F32)
            up = jnp.dot(h, wgu_ref[:, FFN_HIDDEN + c0:FFN_HIDDEN + c1], preferred_element_type=F32)
            act = (gate / (1.0 + jnp.exp(-gate)) * up).astype(BF16)
            acc = acc + jnp.dot(act, wd_ref[c0:c1, :], preferred_element_type=F32)
        z = DEEPNORM_ALPHA * x + g2 * acc
        y_ref[0, rows, :] = _layer_norm(z, g_ref[...], b_ref[...])


def _ffn(x, mod, w_gu_bf16, w_down_bf16, ln_g, ln_b):
    b, s, _ = x.shape
    tm = FFN_SUBTILES * ROW_TILE
    row = pl.BlockSpec((1, tm, D_MODEL), lambda bi, si: (bi, si, 0))
    vec = pl.BlockSpec((1, D_MODEL), lambda bi, si: (0, 0))
    return pl.pallas_call(
        _ffn_kernel,
        grid=(b, s // tm),
        in_specs=[
            row,
            pl.BlockSpec((1, 6, D_MODEL), lambda bi, si: (bi, 0, 0)),
            _resident((D_MODEL, 2 * FFN_HIDDEN), lambda bi, si: (0, 0)),
            _resident((FFN_HIDDEN, D_MODEL), lambda bi, si: (0, 0)),
            vec, vec,
        ],
        out_specs=row,
        out_shape=jax.ShapeDtypeStruct((b, s, D_MODEL), F32),
        compiler_params=_params(("arbitrary", "arbitrary")),
        name="ffn",
    )(x, mod, w_gu_bf16, w_down_bf16, ln_g.reshape(1, D_MODEL), ln_b.reshape(1, D_MODEL))


def _layer(x, mod, layer, tables, w_in_b, bias_tbl, diff_lambda, diff_subln_g, w_out_b,
           ln1_g, ln1_b, w_gu_b, w_down_b, ln2_g, ln2_b):
    lambda_init = 0.8 - 0.6 * math.exp(-0.3 * layer)
    ab, c = _inproj(x, mod, w_in_b, tables)
    oa = _na(ab, bias_tbl)
    ob = _diff(ab, diff_lambda, diff_subln_g, lambda_init)
    oc = _dil(c)
    x = _outproj(oa, ob, oc, x, mod, w_out_b, ln1_g, ln1_b)
    return _ffn(x, mod, w_gu_b, w_down_b, ln2_g, ln2_b)


def kernel(x_prompt, x_sample, c_prompt, c_sample, w_ada, b_ada, w_in, na_rpb, diff_lambda, diff_subln_g,
           w_out, ln1_g, ln1_b, w_gu, w_down, ln2_g, ln2_b):
    xs = [x_prompt, x_sample]
    n_prompt = c_prompt.shape[0]
    mod_all = _ada(jnp.concatenate([c_prompt, c_sample], axis=0), w_ada, b_ada)
    tables = [_rope_tables(x.shape[1], DIFF_QK_DIM) + _rope_tables(x.shape[1], HEAD_DIM) for x in xs]
    for layer in range(DEPTH):
        w_in_b = w_in[layer].astype(BF16)
        w_out_b = w_out[layer].astype(BF16)
        w_gu_b = w_gu[layer].astype(BF16)
        w_down_b = w_down[layer].astype(BF16)
        bias_tbl = _na_bias_table(na_rpb[layer])
        for g in range(2):
            nb = xs[g].shape[0]
            lo = 0 if g == 0 else n_prompt
            mod = mod_all[layer, lo:lo + nb].reshape(nb, 6, D_MODEL)
            xs[g] = _layer(xs[g], mod, layer, tables[g], w_in_b, bias_tbl, diff_lambda[layer],
                           diff_subln_g[layer], w_out_b, ln1_g[layer], ln1_b[layer], w_gu_b, w_down_b,
                           ln2_g[layer], ln2_b[layer])
    return tuple(xs)
```

```python
import functools
import math

import jax
import jax.numpy as jnp
from jax import lax
from jax.experimental import pallas as pl
from jax.experimental.pallas import tpu as pltpu

F32 = jnp.float32
BF16 = jnp.bfloat16

D_MODEL = 1024
DEPTH = 2
HEAD_DIM = 64
NA_HEADS = 4
DIFF_HEADS = 4
DIL_HEADS = 8
NA_WIDTH = NA_HEADS * HEAD_DIM
DIFF_WIDTH = DIFF_HEADS * HEAD_DIM
DIL_WIDTH = DIL_HEADS * HEAD_DIM
MIX_WIDTH = NA_WIDTH + DIFF_WIDTH + DIL_WIDTH
IN_WIDTH = 3 * MIX_WIDTH
AB_WIDTH = 3 * NA_WIDTH + 3 * DIFF_WIDTH
C_WIDTH = 3 * DIL_WIDTH
GRID_W = 64
NA_WIN_ROWS = 8
NA_WIN_COLS = 16
DIFF_QK_DIM = HEAD_DIM // 2
DIL_PATTERNS = ((128, 1), (512, 4), (2048, 16))
DIL_HALF_WIDTH = 64
FFN_HIDDEN = 2816
ROPE_THETA = 10000.0
LN_EPS = 1e-5
DEEPNORM_ALPHA = (2 * DEPTH) ** 0.25
NEG_INF = -1e30
LOG2E = math.log2(math.e)

LANES = 128
MXU_WIDTH = 256
VMEM_LIMIT_BYTES = 48 * 1024 * 1024

ROW_TILE = 512
INPROJ_ROW_TILE = 2 * ROW_TILE
OUTPROJ_SUBTILES = 2
FFN_SUBTILES = 2
NA_ROWS_PER_STEP = 32
DIFF_TQ = 256
DIFF_TK = 512
DIFF_ACC_ROWS = HEAD_DIM + 16
DIL_TILE = 1024
DIL_KEY_LANES = 2 * LANES
DIL_GROUP = 16
FFN_CHUNKS = ((0, 1024), (1024, 2048), (2048, FFN_HIDDEN))

_NT_DIMS = (((1,), (1,)), ((), ()))


def _params(semantics):
    return pltpu.CompilerParams(dimension_semantics=semantics, vmem_limit_bytes=VMEM_LIMIT_BYTES)


def _resident(block_shape, index_map):
    return pl.BlockSpec(block_shape, index_map, pipeline_mode=pl.Buffered(1))


def _ada_kernel(c_ref, w_ref, b_ref, o_ref):
    c = c_ref[...]
    a = c / (1.0 + jnp.exp(-c))
    o_ref[0] = jnp.dot(a, w_ref[0], precision=lax.Precision.HIGHEST,
                       preferred_element_type=F32) + b_ref[0]


def _ada(c_all, w_ada, b_ada):
    nb = c_all.shape[0]
    n_out = w_ada.shape[-1]
    tn = 1536
    return pl.pallas_call(
        _ada_kernel,
        grid=(DEPTH, n_out // tn),
        in_specs=[
            pl.BlockSpec((nb, D_MODEL), lambda l, n: (0, 0)),
            pl.BlockSpec((1, D_MODEL, tn), lambda l, n: (l, 0, n)),
            pl.BlockSpec((1, 1, tn), lambda l, n: (l, 0, n)),
        ],
        out_specs=pl.BlockSpec((1, nb, tn), lambda l, n: (l, 0, n)),
        out_shape=jax.ShapeDtypeStruct((DEPTH, nb, n_out), F32),
        compiler_params=_params(("arbitrary", "arbitrary")),
        name="ada",
    )(c_all, w_ada, b_ada.reshape(DEPTH, 1, n_out))


def _rope_tables(seq, dim):
    half = dim // 2
    inv_freq = ROPE_THETA ** (-jnp.arange(half, dtype=F32) / half)
    ang = jnp.arange(seq, dtype=F32)[:, None] * inv_freq[None, :]
    cos = jnp.cos(ang)
    sin = jnp.sin(ang)
    reps = LANES // dim
    cos_t = jnp.tile(jnp.concatenate([cos, cos], axis=1), (1, reps))
    sin_t = jnp.tile(jnp.concatenate([-sin, sin], axis=1), (1, reps))
    return cos_t, sin_t


def _rope(y, cos, sin_signed, half):
    lane = lax.broadcasted_iota(jnp.int32, y.shape, 1)
    first = (lane & (2 * half - 1)) < half
    partner = jnp.where(first, pltpu.roll(y, LANES - half, axis=1), pltpu.roll(y, half, axis=1))
    return y * cos + partner * sin_signed


_COLUMN_CLASSES = (
    (0, NA_WIDTH, None, HEAD_DIM ** -0.5),
    (NA_WIDTH, 3 * NA_WIDTH, None, None),
    (3 * NA_WIDTH, 3 * NA_WIDTH + DIFF_WIDTH, DIFF_QK_DIM, DIFF_QK_DIM ** -0.5 * LOG2E),
    (3 * NA_WIDTH + DIFF_WIDTH, 3 * NA_WIDTH + 2 * DIFF_WIDTH, DIFF_QK_DIM, None),
    (3 * NA_WIDTH + 2 * DIFF_WIDTH, AB_WIDTH, None, None),
    (AB_WIDTH, AB_WIDTH + DIL_WIDTH, HEAD_DIM, HEAD_DIM ** -0.5 * LOG2E),
    (AB_WIDTH + DIL_WIDTH, AB_WIDTH + 2 * DIL_WIDTH, HEAD_DIM, None),
    (AB_WIDTH + 2 * DIL_WIDTH, IN_WIDTH, None, None),
)


def _column_class(col):
    for lo, hi, rope_dim, scale in _COLUMN_CLASSES:
        if lo <= col < hi:
            return rope_dim, scale
    raise ValueError(col)


def _inproj_kernel(x_ref, mod_ref, w_ref, cos32_ref, sin32_ref, cos64_ref, sin64_ref, ab_ref, c_ref):
    sh1 = mod_ref[0, 0:1, :]
    sc1 = mod_ref[0, 1:2, :]
    h = (x_ref[0] * (1.0 + sc1) + sh1).astype(BF16)
    for n in range(IN_WIDTH // MXU_WIDTH):
        acc = jnp.dot(h, w_ref[:, n * MXU_WIDTH:(n + 1) * MXU_WIDTH], preferred_element_type=F32)
        for part in range(MXU_WIDTH // LANES):
            col = n * MXU_WIDTH + part * LANES
            y = acc[:, part * LANES:(part + 1) * LANES]
            rope_dim, scale = _column_class(col)
            if rope_dim == DIFF_QK_DIM:
                y = _rope(y, cos32_ref[...], sin32_ref[...], rope_dim // 2)
            elif rope_dim == HEAD_DIM:
                y = _rope(y, cos64_ref[...], sin64_ref[...], rope_dim // 2)
            if scale is not None:
                y = y * scale
            if col < AB_WIDTH:
                ab_ref[0, :, col:col + LANES] = y.astype(ab_ref.dtype)
            else:
                c_ref[0, :, col - AB_WIDTH:col - AB_WIDTH + LANES] = y


def _inproj(x, mod, w_in_bf16, tables):
    b, s, _ = x.shape
    tm = INPROJ_ROW_TILE
    table_spec = pl.BlockSpec((tm, LANES), lambda bi, si: (si, 0))
    return pl.pallas_call(
        _inproj_kernel,
        grid=(b, s // tm),
        in_specs=[
            pl.BlockSpec((1, tm, D_MODEL), lambda bi, si: (bi, si, 0)),
            pl.BlockSpec((1, 6, D_MODEL), lambda bi, si: (bi, 0, 0)),
            _resident((D_MODEL, IN_WIDTH), lambda bi, si: (0, 0)),
            table_spec, table_spec, table_spec, table_spec,
        ],
        out_specs=[
            pl.BlockSpec((1, tm, AB_WIDTH), lambda bi, si: (bi, si, 0)),
            pl.BlockSpec((1, tm, C_WIDTH), lambda bi, si: (bi, si, 0)),
        ],
        out_shape=[
            jax.ShapeDtypeStruct((b, s, AB_WIDTH), BF16),
            jax.ShapeDtypeStruct((b, s, C_WIDTH), F32),
        ],
        compiler_params=_params(("arbitrary", "arbitrary")),
        name="inproj",
    )(x, mod, w_in_bf16, *tables)


def _na_bias_table(rpb):
    c_idx = jnp.arange(GRID_W)
    c_start = jnp.clip(c_idx - NA_WIN_COLS // 2, 0, GRID_W - NA_WIN_COLS)
    col_in = (c_idx[None, :] >= c_start[:, None]) & (c_idx[None, :] < c_start[:, None] + NA_WIN_COLS)
    dc = jnp.clip(c_idx[None, :] - c_idx[:, None] + (NA_WIN_COLS - 1), 0, 2 * NA_WIN_COLS - 2)
    v_idx = jnp.arange(NA_WIN_ROWS)
    j_idx = jnp.arange(NA_WIN_ROWS)
    dr = j_idx[None, :] - v_idx[:, None] + (NA_WIN_ROWS - 1)
    bias = rpb.astype(F32)[:, dr][:, :, :, dc]
    bias = jnp.where(col_in[None, None, None], bias, NEG_INF)
    bias = bias.transpose(0, 1, 3, 2, 4)
    bias = bias.reshape(NA_HEADS // 2, 2, NA_WIN_ROWS, GRID_W, NA_WIN_ROWS * GRID_W)
    return bias.transpose(0, 2, 1, 3, 4).reshape(NA_HEADS // 2, NA_WIN_ROWS, 2 * GRID_W, NA_WIN_ROWS * GRID_W)


def _na_kernel(q_ref, k_ref, v_ref, bias_ref, o_ref, *, n_rows):
    step = pl.program_id(2)
    lane = lax.broadcasted_iota(jnp.int32, (GRID_W, LANES), 1)
    first = lane < HEAD_DIM
    win = NA_WIN_ROWS * GRID_W
    scores, vws = [], []
    for rr in range(NA_ROWS_PER_STEP):
        r = step * NA_ROWS_PER_STEP + rr
        row0 = jnp.clip(r - NA_WIN_ROWS // 2, 0, n_rows - NA_WIN_ROWS)
        start = pl.multiple_of(row0 * GRID_W, GRID_W)
        kw = k_ref[0, pl.ds(start, win), :]
        vws.append(v_ref[0, pl.ds(start, win), :])
        q = q_ref[0, rr * GRID_W:(rr + 1) * GRID_W, :]
        zero = jnp.zeros_like(q)
        q2 = jnp.concatenate([jnp.where(first, q, zero), jnp.where(first, zero, q)], axis=0)
        scores.append(lax.dot_general(q2, kw, _NT_DIMS, preferred_element_type=F32) + bias_ref[0, r - row0])
    probs, dens = [], []
    for s in scores:
        p = jnp.exp(s - jnp.max(s, axis=1, keepdims=True))
        dens.append(jnp.sum(p, axis=1, keepdims=True))
        probs.append(p.astype(BF16))
    for rr, (p, l, vw) in enumerate(zip(probs, dens, vws)):
        o2 = jnp.dot(p, vw, preferred_element_type=F32) / l
        o_ref[0, rr * GRID_W:(rr + 1) * GRID_W, :] = jnp.where(first, o2[:GRID_W], o2[GRID_W:]).astype(o_ref.dtype)


def _na(ab, bias_tbl):
    b, s, _ = ab.shape
    n_rows = s // GRID_W
    tq = NA_ROWS_PER_STEP * GRID_W
    k_blk = NA_WIDTH // LANES
    return pl.pallas_call(
        functools.partial(_na_kernel, n_rows=n_rows),
        grid=(b, NA_HEADS // 2, s // tq),
        in_specs=[
            pl.BlockSpec((1, tq, LANES), lambda bi, hp, i: (bi, i, hp)),
            pl.BlockSpec((1, s, LANES), lambda bi, hp, i: (bi, 0, k_blk + hp)),
            pl.BlockSpec((1, s, LANES), lambda bi, hp, i: (bi, 0, 2 * k_blk + hp)),
            pl.BlockSpec((1, NA_WIN_ROWS, 2 * GRID_W, NA_WIN_ROWS * GRID_W), lambda bi, hp, i: (hp, 0, 0, 0)),
        ],
        out_specs=pl.BlockSpec((1, tq, LANES), lambda bi, hp, i: (bi, i, hp)),
        out_shape=jax.ShapeDtypeStruct((b, s, NA_WIDTH), BF16),
        compiler_params=_params(("arbitrary", "arbitrary", "arbitrary")),
        name="na",
    )(ab, ab, ab, bias_tbl)


def _diff_kernel(q_ref, k_ref, v_ref, lam_ref, g_ref, o_ref, vt_scr, acc_scr, s_even, s_odd, qm_scr, *,
                 seq, lambda_init):
    n_kv = seq // DIFF_TK
    n_q = seq // DIFF_TQ
    assert n_kv % 2 == 0, "the last kv tile must read s_odd so that s_even is free for the next query tile"
    q_tile = pl.program_id(2)
    feat = lax.broadcasted_iota(jnp.int32, (LANES, DIFF_TQ), 0)
    group = lax.shift_right_logical(feat, DIFF_QK_DIM.bit_length() - 1)

    def masked_qt(tile_idx):
        start = pl.multiple_of(tile_idx * DIFF_TQ, DIFF_TQ)
        qt = q_ref[0, pl.ds(start, DIFF_TQ), :].astype(F32).T
        return [jnp.where(group == c, qt, 0.0).astype(BF16) for c in range(4)]

    def scores(t, c, qm, dst):
        start = pl.multiple_of(t * DIFF_TK, DIFF_TK)
        dst[c] = jnp.dot(k_ref[0, pl.ds(start, DIFF_TK), :], qm, preferred_element_type=F32)

    @pl.when(q_tile == 0)
    def _():
        ones = jnp.ones((DIFF_ACC_ROWS - HEAD_DIM, DIFF_TK), BF16)
        for t in range(n_kv):
            vt = v_ref[0, t * DIFF_TK:(t + 1) * DIFF_TK, :].astype(F32).T.astype(BF16)
            for h in range(2):
                vt_scr[t, h, 0:HEAD_DIM, :] = vt[h * HEAD_DIM:(h + 1) * HEAD_DIM]
                vt_scr[t, h, HEAD_DIM:DIFF_ACC_ROWS, :] = ones
        for c, qm in enumerate(masked_qt(0)):
            qm_scr[c] = qm
            scores(0, c, qm, s_even)

    qmts = [qm_scr[c] for c in range(4)]
    acc_scr[...] = jnp.zeros(acc_scr.shape, F32)

    def consume(t, ms, cur, nxt):
        last = t + 1 == n_kv
        if last:
            next_qms = masked_qt(jnp.minimum(q_tile + 1, n_q - 1))
        new_ms = []
        for c in range(4):
            if last:
                qm_scr[c] = next_qms[c]
                scores(0, c, next_qms[c], nxt)
            else:
                scores(t + 1, c, qmts[c], nxt)
            st = cur[c]
            m_new = jnp.maximum(ms[c], jnp.max(st, axis=0, keepdims=True))
            p = jnp.exp2(st - m_new).astype(BF16)
            alpha = jnp.exp2(ms[c] - m_new)
            acc_scr[c] = alpha * acc_scr[c] + jnp.dot(vt_scr[t, c // 2], p, preferred_element_type=F32)
            new_ms.append(m_new)
        return tuple(new_ms)

    ms = tuple(jnp.full((1, DIFF_TQ), NEG_INF, F32) for _ in range(4))
    for t in range(n_kv):
        cur, nxt = (s_even, s_odd) if t % 2 == 0 else (s_odd, s_even)
        ms = consume(t, ms, cur, nxt)

    lf = lam_ref[...]
    lam = (jnp.exp(jnp.sum(lf[0:1] * lf[1:2], axis=1, keepdims=True))
           - jnp.exp(jnp.sum(lf[2:3] * lf[3:4], axis=1, keepdims=True)) + lambda_init)
    o = [acc_scr[c, 0:HEAD_DIM, :] / acc_scr[c, HEAD_DIM:HEAD_DIM + 1, :] for c in range(4)]
    normed = []
    for h in range(2):
        d = o[2 * h] - lam * o[2 * h + 1]
        ms_h = jnp.sum(d * d, axis=0, keepdims=True) * (1.0 / HEAD_DIM)
        normed.append(d * lax.rsqrt(ms_h + LN_EPS))
    out = jnp.concatenate(normed, axis=0).T
    o_ref[0] = (out * g_ref[...] * (1.0 - lambda_init)).astype(o_ref.dtype)


def _diff(ab, lam_vecs, subln_g, lambda_init):
    b, s, _ = ab.shape
    q_blk = 3 * NA_WIDTH // LANES
    k_blk = q_blk + DIFF_WIDTH // LANES
    v_blk = k_blk + DIFF_WIDTH // LANES
    g_tile = jnp.tile(subln_g.astype(F32), LANES // HEAD_DIM).reshape(1, LANES)
    return pl.pallas_call(
        functools.partial(_diff_kernel, seq=s, lambda_init=lambda_init),
        grid=(b, DIFF_HEADS // 2, s // DIFF_TQ),
        in_specs=[
            pl.BlockSpec((1, s, LANES), lambda bi, hp, i: (bi, 0, q_blk + hp)),
            pl.BlockSpec((1, s, LANES), lambda bi, hp, i: (bi, 0, k_blk + hp)),
            pl.BlockSpec((1, s, LANES), lambda bi, hp, i: (bi, 0, v_blk + hp)),
            pl.BlockSpec((4, DIFF_QK_DIM), lambda bi, hp, i: (0, 0)),
            pl.BlockSpec((1, LANES), lambda bi, hp, i: (0, 0)),
        ],
        out_specs=pl.BlockSpec((1, DIFF_TQ, LANES), lambda bi, hp, i: (bi, i, hp)),
        out_shape=jax.ShapeDtypeStruct((b, s, DIFF_WIDTH), BF16),
        scratch_shapes=[
            pltpu.VMEM((s // DIFF_TK, 2, DIFF_ACC_ROWS, DIFF_TK), BF16),
            pltpu.VMEM((4, DIFF_ACC_ROWS, DIFF_TQ), F32),
            pltpu.VMEM((4, DIFF_TK, DIFF_TQ), F32),
            pltpu.VMEM((4, DIFF_TK, DIFF_TQ), F32),
            pltpu.VMEM((4, LANES, DIFF_TQ), BF16),
        ],
        compiler_params=_params(("arbitrary", "arbitrary", "arbitrary")),
        name="diff",
    )(ab, ab, ab, lam_vecs.astype(F32), g_tile)


def _dil_kernel(q_ref, k_ref, v_ref, o_ref, o_scr, m_scr, l_scr, bias_scr, *, seq):
    tile = pl.program_id(2)
    hw = DIL_HALF_WIDTH
    lane = lax.broadcasted_iota(jnp.int32, (hw, LANES), 1)
    first = lane < HEAD_DIM
    row = lax.broadcasted_iota(jnp.int32, (2 * hw, DIL_KEY_LANES), 0) & (hw - 1)
    col = lax.broadcasted_iota(jnp.int32, (2 * hw, DIL_KEY_LANES), 1)
    pad = jnp.zeros((DIL_KEY_LANES - 3 * hw, LANES), BF16)
    delta = col - hw - row
    band = jnp.minimum(delta + hw, hw - delta)
    for variant in range(4):
        col_min = hw if variant & 1 else 0
        col_max = (2 * hw if variant & 2 else 3 * hw) - 1
        ok = jnp.minimum(band, jnp.minimum(col - col_min, col_max - col)) >= 0
        bias_scr[variant] = jnp.where(ok, 0.0, NEG_INF)

    for pat, (_, dil) in enumerate(DIL_PATTERNS):
        blocks_per_tile = DIL_TILE // (hw * dil)
        n_blocks = seq // (hw * dil)
        units = [(u % dil, u // dil) for u in range(DIL_TILE // hw)]
        for g0 in range(0, len(units), DIL_GROUP):
            group = units[g0:g0 + DIL_GROUP]
            scores, vcats = [], []
            for res, lb in group:
                blk = tile * blocks_per_tile + lb
                q = q_ref[0, pl.ds(lb * (hw * dil) + res, hw, stride=dil), :].astype(BF16)
                zero = jnp.zeros_like(q)
                q2 = jnp.concatenate([jnp.where(first, q, zero), jnp.where(first, zero, q)], axis=0)
                ks, vs = [], []
                for j in (-1, 0, 1):
                    kb = jnp.clip(blk + j, 0, n_blocks - 1)
                    k_start = kb * (hw * dil) + res
                    ks.append(k_ref[0, pl.ds(k_start, hw, stride=dil), :])
                    vs.append(v_ref[0, pl.ds(k_start, hw, stride=dil), :])
                kcat = jnp.concatenate([jnp.concatenate(ks, axis=0).astype(BF16), pad], axis=0)
                vcats.append(jnp.concatenate([jnp.concatenate(vs, axis=0).astype(BF16), pad], axis=0))
                s = lax.dot_general(q2, kcat, _NT_DIMS, preferred_element_type=F32)
                variant = jnp.where(blk > 0, 0, 1) + jnp.where(blk < n_blocks - 1, 0, 2)
                scores.append(s + bias_scr[variant])
            probs, stats = [], []
            for s in scores:
                m = jnp.max(s, axis=1, keepdims=True)
                p = jnp.exp2(s - m)
                stats.append((m, jnp.sum(p, axis=1, keepdims=True)))
                probs.append(p.astype(BF16))
            for (res, lb), p, vcat, (m, l) in zip(group, probs, vcats, stats):
                o2 = jnp.dot(p, vcat, preferred_element_type=F32)
                rows = pl.ds(lb * (hw * dil) + res, hw, stride=dil)
                o_scr[pat, rows, :] = jnp.where(first, o2[:hw], o2[hw:])
                m_scr[pat, rows, :] = jnp.where(first, m[:hw], m[hw:])
                l_scr[pat, rows, :] = jnp.where(first, l[:hw], l[hw:])

    m_all = jnp.maximum(jnp.maximum(m_scr[0], m_scr[1]), m_scr[2])
    num = jnp.zeros((DIL_TILE, LANES), F32)
    den = jnp.zeros((DIL_TILE, LANES), F32)
    for pat in range(len(DIL_PATTERNS)):
        w = jnp.exp2(m_scr[pat] - m_all)
        num = num + w * o_scr[pat]
        den = den + w * l_scr[pat]
    o_ref[0] = (num / den).astype(o_ref.dtype)


def _dil(c):
    b, s, _ = c.shape
    k_blk = DIL_WIDTH // LANES
    n_pat = len(DIL_PATTERNS)
    return pl.pallas_call(
        functools.partial(_dil_kernel, seq=s),
        grid=(b, DIL_HEADS // 2, s // DIL_TILE),
        in_specs=[
            pl.BlockSpec((1, DIL_TILE, LANES), lambda bi, hp, i: (bi, i, hp)),
            pl.BlockSpec((1, s, LANES), lambda bi, hp, i: (bi, 0, k_blk + hp)),
            pl.BlockSpec((1, s, LANES), lambda bi, hp, i: (bi, 0, 2 * k_blk + hp)),
        ],
        out_specs=pl.BlockSpec((1, DIL_TILE, LANES), lambda bi, hp, i: (bi, i, hp)),
        out_shape=jax.ShapeDtypeStruct((b, s, DIL_WIDTH), BF16),
        scratch_shapes=[
            pltpu.VMEM((n_pat, DIL_TILE, LANES), F32),
            pltpu.VMEM((n_pat, DIL_TILE, LANES), F32),
            pltpu.VMEM((n_pat, DIL_TILE, LANES), F32),
            pltpu.VMEM((4, 2 * DIL_HALF_WIDTH, DIL_KEY_LANES), F32),
        ],
        compiler_params=_params(("arbitrary", "arbitrary", "arbitrary")),
        name="dil",
    )(c, c, c)


def _layer_norm(z, g, b):
    mu = jnp.mean(z, axis=1, keepdims=True)
    zc = z - mu
    var = jnp.mean(zc * zc, axis=1, keepdims=True)
    return zc * lax.rsqrt(var + LN_EPS) * g + b


def _outproj_kernel(oa_ref, ob_ref, oc_ref, x_ref, mod_ref, w_ref, g_ref, b_ref, y_ref):
    g1 = mod_ref[0, 2:3, :]
    for sub in range(OUTPROJ_SUBTILES):
        rows = slice(sub * ROW_TILE, (sub + 1) * ROW_TILE)
        mix = jnp.dot(oa_ref[0, rows, :], w_ref[0:NA_WIDTH, :], preferred_element_type=F32)
        mix = mix + jnp.dot(ob_ref[0, rows, :], w_ref[NA_WIDTH:NA_WIDTH + DIFF_WIDTH, :],
                            preferred_element_type=F32)
        mix = mix + jnp.dot(oc_ref[0, rows, :], w_ref[NA_WIDTH + DIFF_WIDTH:MIX_WIDTH, :],
                            preferred_element_type=F32)
        z = DEEPNORM_ALPHA * x_ref[0, rows, :] + g1 * mix
        y_ref[0, rows, :] = _layer_norm(z, g_ref[...], b_ref[...])


def _outproj(oa, ob, oc, x, mod, w_out_bf16, ln_g, ln_b):
    b, s, _ = x.shape
    tm = OUTPROJ_SUBTILES * ROW_TILE
    row = lambda width: pl.BlockSpec((1, tm, width), lambda bi, si: (bi, si, 0))
    vec = pl.BlockSpec((1, D_MODEL), lambda bi, si: (0, 0))
    return pl.pallas_call(
        _outproj_kernel,
        grid=(b, s // tm),
        in_specs=[
            row(NA_WIDTH), row(DIFF_WIDTH), row(DIL_WIDTH), row(D_MODEL),
            pl.BlockSpec((1, 6, D_MODEL), lambda bi, si: (bi, 0, 0)),
            _resident((MIX_WIDTH, D_MODEL), lambda bi, si: (0, 0)),
            vec, vec,
        ],
        out_specs=row(D_MODEL),
        out_shape=jax.ShapeDtypeStruct((b, s, D_MODEL), F32),
        compiler_params=_params(("arbitrary", "arbitrary")),
        name="outproj",
    )(oa, ob, oc, x, mod, w_out_bf16, ln_g.reshape(1, D_MODEL), ln_b.reshape(1, D_MODEL))


def _ffn_kernel(x_ref, mod_ref, wgu_ref, wd_ref, g_ref, b_ref, y_ref):
    sh2 = mod_ref[0, 3:4, :]
    sc2 = mod_ref[0, 4:5, :]
    g2 = mod_ref[0, 5:6, :]
    for sub in range(FFN_SUBTILES):
        rows = slice(sub * ROW_TILE, (sub + 1) * ROW_TILE)
        x = x_ref[0, rows, :]
        h = (x * (1.0 + sc2) + sh2).astype(BF16)
        acc = jnp.zeros((ROW_TILE, D_MODEL), F32)
        for c0, c1 in FFN_CHUNKS:
            gate = jnp.dot(h, wgu_ref[:, c0:c1], preferred_element_type=F32)
            up = jnp.dot(h, wgu_ref[:, FFN_HIDDEN + c0:FFN_HIDDEN + c1], preferred_element_type=F32)
            act = (gate / (1.0 + jnp.exp(-gate)) * up).astype(BF16)
            acc = acc + jnp.dot(act, wd_ref[c0:c1, :], preferred_element_type=F32)
        z = DEEPNORM_ALPHA * x + g2 * acc
        y_ref[0, rows, :] = _layer_norm(z, g_ref[...], b_ref[...])


def _ffn(x, mod, w_gu_bf16, w_down_bf16, ln_g, ln_b):
    b, s, _ = x.shape
    tm = FFN_SUBTILES * ROW_TILE
    row = pl.BlockSpec((1, tm, D_MODEL), lambda bi, si: (bi, si, 0))
    vec = pl.BlockSpec((1, D_MODEL), lambda bi, si: (0, 0))
    return pl.pallas_call(
        _ffn_kernel,
        grid=(b, s // tm),
        in_specs=[
            row,
            pl.BlockSpec((1, 6, D_MODEL), lambda bi, si: (bi, 0, 0)),
            _resident((D_MODEL, 2 * FFN_HIDDEN), lambda bi, si: (0, 0)),
            _resident((FFN_HIDDEN, D_MODEL), lambda bi, si: (0, 0)),
            vec, vec,
        ],
        out_specs=row,
        out_shape=jax.ShapeDtypeStruct((b, s, D_MODEL), F32),
        compiler_params=_params(("arbitrary", "arbitrary")),
        name="ffn",
    )(x, mod, w_gu_bf16, w_down_bf16, ln_g.reshape(1, D_MODEL), ln_b.reshape(1, D_MODEL))


def _layer(x, mod, layer, tables, w_in_b, bias_tbl, diff_lambda, diff_subln_g, w_out_b,
           ln1_g, ln1_b, w_gu_b, w_down_b, ln2_g, ln2_b):
    lambda_init = 0.8 - 0.6 * math.exp(-0.3 * layer)
    ab, c = _inproj(x, mod, w_in_b, tables)
    oa = _na(ab, bias_tbl)
    ob = _diff(ab, diff_lambda, diff_subln_g, lambda_init)
    oc = _dil(c)
    x = _outproj(oa, ob, oc, x, mod, w_out_b, ln1_g, ln1_b)
    return _ffn(x, mod, w_gu_b, w_down_b, ln2_g, ln2_b)


def kernel(x_prompt, x_sample, c_prompt, c_sample, w_ada, b_ada, w_in, na_rpb, diff_lambda, diff_subln_g,
           w_out, ln1_g, ln1_b, w_gu, w_down, ln2_g, ln2_b):
    xs = [x_prompt, x_sample]
    n_prompt = c_prompt.shape[0]
    mod_all = _ada(jnp.concatenate([c_prompt, c_sample], axis=0), w_ada, b_ada)
    tables = [_rope_tables(x.shape[1], DIFF_QK_DIM) + _rope_tables(x.shape[1], HEAD_DIM) for x in xs]
    for layer in range(DEPTH):
        w_in_b = w_in[layer].astype(BF16)
        w_out_b = w_out[layer].astype(BF16)
        w_gu_b = w_gu[layer].astype(BF16)
        w_down_b = w_down[layer].astype(BF16)
        bias_tbl = _na_bias_table(na_rpb[layer])
        for g in range(2):
            nb = xs[g].shape[0]
            lo = 0 if g == 0 else n_prompt
            mod = mod_all[layer, lo:lo + nb].reshape(nb, 6, D_MODEL)
            xs[g] = _layer(xs[g], mod, layer, tables[g], w_in_b, bias_tbl, diff_lambda[layer],
                           diff_subln_g[layer], w_out_b, ln1_g[layer], ln1_b[layer], w_gu_b, w_down_b,
                           ln2_g[layer], ln2_b[layer])
    return tuple(xs)
```

```python
import functools
import math

import jax
import jax.numpy as jnp
from jax import lax
from jax.experimental import pallas as pl
from jax.experimental.pallas import tpu as pltpu

F32 = jnp.float32
BF16 = jnp.bfloat16

D_MODEL = 1024
DEPTH = 2
HEAD_DIM = 64
NA_HEADS = 4
DIFF_HEADS = 4
DIL_HEADS = 8
NA_WIDTH = NA_HEADS * HEAD_DIM
DIFF_WIDTH = DIFF_HEADS * HEAD_DIM
DIL_WIDTH = DIL_HEADS * HEAD_DIM
MIX_WIDTH = NA_WIDTH + DIFF_WIDTH + DIL_WIDTH
IN_WIDTH = 3 * MIX_WIDTH
AB_WIDTH = 3 * NA_WIDTH + 3 * DIFF_WIDTH
C_WIDTH = 3 * DIL_WIDTH
GRID_W = 64
NA_WIN_ROWS = 8
NA_WIN_COLS = 16
DIFF_QK_DIM = HEAD_DIM // 2
DIL_PATTERNS = ((128, 1), (512, 4), (2048, 16))
DIL_HALF_WIDTH = 64
FFN_HIDDEN = 2816
ROPE_THETA = 10000.0
LN_EPS = 1e-5
DEEPNORM_ALPHA = (2 * DEPTH) ** 0.25
NEG_INF = -1e30
LOG2E = math.log2(math.e)

LANES = 128
MXU_WIDTH = 256
VMEM_LIMIT_BYTES = 48 * 1024 * 1024

ROW_TILE = 512
INPROJ_ROW_TILE = 2 * ROW_TILE
OUTPROJ_SUBTILES = 2
FFN_SUBTILES = 2
NA_ROWS_PER_STEP = 32
DIFF_TQ = 256
DIFF_TK = 512
DIFF_ACC_ROWS = HEAD_DIM + 16
DIL_TILE = 1024
DIL_KEY_LANES = 2 * LANES
DIL_GROUP = 16
FFN_CHUNKS = ((0, 1024), (1024, 2048), (2048, FFN_HIDDEN))

_NT_DIMS = (((1,), (1,)), ((), ()))


def _params(semantics):
    return pltpu.CompilerParams(dimension_semantics=semantics, vmem_limit_bytes=VMEM_LIMIT_BYTES)


def _resident(block_shape, index_map):
    return pl.BlockSpec(block_shape, index_map, pipeline_mode=pl.Buffered(1))


def _ada_kernel(c_ref, w_ref, b_ref, o_ref):
    c = c_ref[...]
    a = c / (1.0 + jnp.exp(-c))
    o_ref[0] = jnp.dot(a, w_ref[0], precision=lax.Precision.HIGHEST,
                       preferred_element_type=F32) + b_ref[0]


def _ada(c_all, w_ada, b_ada):
    nb = c_all.shape[0]
    n_out = w_ada.shape[-1]
    tn = 1536
    return pl.pallas_call(
        _ada_kernel,
        grid=(DEPTH, n_out // tn),
        in_specs=[
            pl.BlockSpec((nb, D_MODEL), lambda l, n: (0, 0)),
            pl.BlockSpec((1, D_MODEL, tn), lambda l, n: (l, 0, n)),
            pl.BlockSpec((1, 1, tn), lambda l, n: (l, 0, n)),
        ],
        out_specs=pl.BlockSpec((1, nb, tn), lambda l, n: (l, 0, n)),
        out_shape=jax.ShapeDtypeStruct((DEPTH, nb, n_out), F32),
        compiler_params=_params(("arbitrary", "arbitrary")),
        name="ada",
    )(c_all, w_ada, b_ada.reshape(DEPTH, 1, n_out))


def _rope_tables(seq, dim):
    half = dim // 2
    inv_freq = ROPE_THETA ** (-jnp.arange(half, dtype=F32) / half)
    ang = jnp.arange(seq, dtype=F32)[:, None] * inv_freq[None, :]
    cos = jnp.cos(ang)
    sin = jnp.sin(ang)
    reps = LANES // dim
    cos_t = jnp.tile(jnp.concatenate([cos, cos], axis=1), (1, reps))
    sin_t = jnp.tile(jnp.concatenate([-sin, sin], axis=1), (1, reps))
    return cos_t, sin_t


def _rope(y, cos, sin_signed, half):
    lane = lax.broadcasted_iota(jnp.int32, y.shape, 1)
    first = (lane & (2 * half - 1)) < half
    partner = jnp.where(first, pltpu.roll(y, LANES - half, axis=1), pltpu.roll(y, half, axis=1))
    return y * cos + partner * sin_signed


_COLUMN_CLASSES = (
    (0, NA_WIDTH, None, HEAD_DIM ** -0.5),
    (NA_WIDTH, 3 * NA_WIDTH, None, None),
    (3 * NA_WIDTH, 3 * NA_WIDTH + DIFF_WIDTH, DIFF_QK_DIM, DIFF_QK_DIM ** -0.5 * LOG2E),
    (3 * NA_WIDTH + DIFF_WIDTH, 3 * NA_WIDTH + 2 * DIFF_WIDTH, DIFF_QK_DIM, None),
    (3 * NA_WIDTH + 2 * DIFF_WIDTH, AB_WIDTH, None, None),
    (AB_WIDTH, AB_WIDTH + DIL_WIDTH, HEAD_DIM, HEAD_DIM ** -0.5 * LOG2E),
    (AB_WIDTH + DIL_WIDTH, AB_WIDTH + 2 * DIL_WIDTH, HEAD_DIM, None),
    (AB_WIDTH + 2 * DIL_WIDTH, IN_WIDTH, None, None),
)


def _column_class(col):
    for lo, hi, rope_dim, scale in _COLUMN_CLASSES:
        if lo <= col < hi:
            return rope_dim, scale
    raise ValueError(col)


def _inproj_kernel(x_ref, mod_ref, w_ref, cos32_ref, sin32_ref, cos64_ref, sin64_ref, ab_ref, c_ref):
    sh1 = mod_ref[0, 0:1, :]
    sc1 = mod_ref[0, 1:2, :]
    h = (x_ref[0] * (1.0 + sc1) + sh1).astype(BF16)
    for n in range(IN_WIDTH // MXU_WIDTH):
        acc = jnp.dot(h, w_ref[:, n * MXU_WIDTH:(n + 1) * MXU_WIDTH], preferred_element_type=F32)
        for part in range(MXU_WIDTH // LANES):
            col = n * MXU_WIDTH + part * LANES
            y = acc[:, part * LANES:(part + 1) * LANES]
            rope_dim, scale = _column_class(col)
            if rope_dim == DIFF_QK_DIM:
                y = _rope(y, cos32_ref[...], sin32_ref[...], rope_dim // 2)
            elif rope_dim == HEAD_DIM:
                y = _rope(y, cos64_ref[...], sin64_ref[...], rope_dim // 2)
            if scale is not None:
                y = y * scale
            if col < AB_WIDTH:
                ab_ref[0, :, col:col + LANES] = y.astype(ab_ref.dtype)
            else:
                c_ref[0, :, col - AB_WIDTH:col - AB_WIDTH + LANES] = y


def _inproj(x, mod, w_in_bf16, tables):
    b, s, _ = x.shape
    tm = INPROJ_ROW_TILE
    table_spec = pl.BlockSpec((tm, LANES), lambda bi, si: (si, 0))
    return pl.pallas_call(
        _inproj_kernel,
        grid=(b, s // tm),
        in_specs=[
            pl.BlockSpec((1, tm, D_MODEL), lambda bi, si: (bi, si, 0)),
            pl.BlockSpec((1, 6, D_MODEL), lambda bi, si: (bi, 0, 0)),
            _resident((D_MODEL, IN_WIDTH), lambda bi, si: (0, 0)),
            table_spec, table_spec, table_spec, table_spec,
        ],
        out_specs=[
            pl.BlockSpec((1, tm, AB_WIDTH), lambda bi, si: (bi, si, 0)),
            pl.BlockSpec((1, tm, C_WIDTH), lambda bi, si: (bi, si, 0)),
        ],
        out_shape=[
            jax.ShapeDtypeStruct((b, s, AB_WIDTH), BF16),
            jax.ShapeDtypeStruct((b, s, C_WIDTH), F32),
        ],
        compiler_params=_params(("arbitrary", "arbitrary")),
        name="inproj",
    )(x, mod, w_in_bf16, *tables)


def _na_bias_table(rpb):
    c_idx = jnp.arange(GRID_W)
    c_start = jnp.clip(c_idx - NA_WIN_COLS // 2, 0, GRID_W - NA_WIN_COLS)
    col_in = (c_idx[None, :] >= c_start[:, None]) & (c_idx[None, :] < c_start[:, None] + NA_WIN_COLS)
    dc = jnp.clip(c_idx[None, :] - c_idx[:, None] + (NA_WIN_COLS - 1), 0, 2 * NA_WIN_COLS - 2)
    v_idx = jnp.arange(NA_WIN_ROWS)
    j_idx = jnp.arange(NA_WIN_ROWS)
    dr = j_idx[None, :] - v_idx[:, None] + (NA_WIN_ROWS - 1)
    bias = rpb.astype(F32)[:, dr][:, :, :, dc]
    bias = jnp.where(col_in[None, None, None], bias, NEG_INF)
    bias = bias.transpose(0, 1, 3, 2, 4)
    bias = bias.reshape(NA_HEADS // 2, 2, NA_WIN_ROWS, GRID_W, NA_WIN_ROWS * GRID_W)
    return bias.transpose(0, 2, 1, 3, 4).reshape(NA_HEADS // 2, NA_WIN_ROWS, 2 * GRID_W, NA_WIN_ROWS * GRID_W)


def _na_kernel(q_ref, k_ref, v_ref, bias_ref, o_ref, *, n_rows):
    step = pl.program_id(2)
    lane = lax.broadcasted_iota(jnp.int32, (GRID_W, LANES), 1)
    first = lane < HEAD_DIM
    win = NA_WIN_ROWS * GRID_W
    scores, vws = [], []
    for rr in range(NA_ROWS_PER_STEP):
        r = step * NA_ROWS_PER_STEP + rr
        row0 = jnp.clip(r - NA_WIN_ROWS // 2, 0, n_rows - NA_WIN_ROWS)
        start = pl.multiple_of(row0 * GRID_W, GRID_W)
        kw = k_ref[0, pl.ds(start, win), :]
        vws.append(v_ref[0, pl.ds(start, win), :])
        q = q_ref[0, rr * GRID_W:(rr + 1) * GRID_W, :]
        zero = jnp.zeros_like(q)
        q2 = jnp.concatenate([jnp.where(first, q, zero), jnp.where(first, zero, q)], axis=0)
        scores.append(lax.dot_general(q2, kw, _NT_DIMS, preferred_element_type=F32) + bias_ref[0, r - row0])
    probs, dens = [], []
    for s in scores:
        p = jnp.exp(s - jnp.max(s, axis=1, keepdims=True))
        dens.append(jnp.sum(p, axis=1, keepdims=True))
        probs.append(p.astype(BF16))
    for rr, (p, l, vw) in enumerate(zip(probs, dens, vws)):
        o2 = jnp.dot(p, vw, preferred_element_type=F32) / l
        o_ref[0, rr * GRID_W:(rr + 1) * GRID_W, :] = jnp.where(first, o2[:GRID_W], o2[GRID_W:]).astype(o_ref.dtype)


def _na(ab, bias_tbl):
    b, s, _ = ab.shape
    n_rows = s // GRID_W
    tq = NA_ROWS_PER_STEP * GRID_W
    k_blk = NA_WIDTH // LANES
    return pl.pallas_call(
        functools.partial(_na_kernel, n_rows=n_rows),
        grid=(b, NA_HEADS // 2, s // tq),
        in_specs=[
            pl.BlockSpec((1, tq, LANES), lambda bi, hp, i: (bi, i, hp)),
            pl.BlockSpec((1, s, LANES), lambda bi, hp, i: (bi, 0, k_blk + hp)),
            pl.BlockSpec((1, s, LANES), lambda bi, hp, i: (bi, 0, 2 * k_blk + hp)),
            pl.BlockSpec((1, NA_WIN_ROWS, 2 * GRID_W, NA_WIN_ROWS * GRID_W), lambda bi, hp, i: (hp, 0, 0, 0)),
        ],
        out_specs=pl.BlockSpec((1, tq, LANES), lambda bi, hp, i: (bi, i, hp)),
        out_shape=jax.ShapeDtypeStruct((b, s, NA_WIDTH), BF16),
        compiler_params=_params(("arbitrary", "arbitrary", "arbitrary")),
        name="na",
    )(ab, ab, ab, bias_tbl)


def _diff_kernel(q_ref, k_ref, v_ref, lam_ref, g_ref, o_ref, vt_scr, acc_scr, s_even, s_odd, qm_scr, *,
                 seq, lambda_init):
    n_kv = seq // DIFF_TK
    n_q = seq // DIFF_TQ
    assert n_kv % 2 == 0, "the last kv tile must read s_odd so that s_even is free for the next query tile"
    q_tile = pl.program_id(2)
    feat = lax.broadcasted_iota(jnp.int32, (LANES, DIFF_TQ), 0)
    group = lax.shift_right_logical(feat, DIFF_QK_DIM.bit_length() - 1)

    def masked_qt(tile_idx):
        start = pl.multiple_of(tile_idx * DIFF_TQ, DIFF_TQ)
        qt = q_ref[0, pl.ds(start, DIFF_TQ), :].astype(F32).T
        return [jnp.where(group == c, qt, 0.0).astype(BF16) for c in range(4)]

    def scores(t, c, qm, dst):
        start = pl.multiple_of(t * DIFF_TK, DIFF_TK)
        dst[c] = jnp.dot(k_ref[0, pl.ds(start, DIFF_TK), :], qm, preferred_element_type=F32)

    @pl.when(q_tile == 0)
    def _():
        ones = jnp.ones((DIFF_ACC_ROWS - HEAD_DIM, DIFF_TK), BF16)
        for t in range(n_kv):
            vt = v_ref[0, t * DIFF_TK:(t + 1) * DIFF_TK, :].astype(F32).T.astype(BF16)
            for h in range(2):
                vt_scr[t, h, 0:HEAD_DIM, :] = vt[h * HEAD_DIM:(h + 1) * HEAD_DIM]
                vt_scr[t, h, HEAD_DIM:DIFF_ACC_ROWS, :] = ones
        for c, qm in enumerate(masked_qt(0)):
            qm_scr[c] = qm
            scores(0, c, qm, s_even)

    qmts = [qm_scr[c] for c in range(4)]
    acc_scr[...] = jnp.zeros(acc_scr.shape, F32)

    def consume(t, ms, cur, nxt):
        last = t + 1 == n_kv
        if last:
            next_qms = masked_qt(jnp.minimum(q_tile + 1, n_q - 1))
        new_ms = []
        for c in range(4):
            if last:
                qm_scr[c] = next_qms[c]
                scores(0, c, next_qms[c], nxt)
            else:
                scores(t + 1, c, qmts[c], nxt)
            st = cur[c]
            m_new = jnp.maximum(ms[c], jnp.max(st, axis=0, keepdims=True))
            p = jnp.exp2(st - m_new).astype(BF16)
            alpha = jnp.exp2(ms[c] - m_new)
            acc_scr[c] = alpha * acc_scr[c] + jnp.dot(vt_scr[t, c // 2], p, preferred_element_type=F32)
            new_ms.append(m_new)
        return tuple(new_ms)

    ms = tuple(jnp.full((1, DIFF_TQ), NEG_INF, F32) for _ in range(4))
    for t in range(n_kv):
        cur, nxt = (s_even, s_odd) if t % 2 == 0 else (s_odd, s_even)
        ms = consume(t, ms, cur, nxt)

    lf = lam_ref[...]
    lam = (jnp.exp(jnp.sum(lf[0:1] * lf[1:2], axis=1, keepdims=True))
           - jnp.exp(jnp.sum(lf[2:3] * lf[3:4], axis=1, keepdims=True)) + lambda_init)
    o = [acc_scr[c, 0:HEAD_DIM, :] / acc_scr[c, HEAD_DIM:HEAD_DIM + 1, :] for c in range(4)]
    normed = []
    for h in range(2):
        d = o[2 * h] - lam * o[2 * h + 1]
        ms_h = jnp.sum(d * d, axis=0, keepdims=True) * (1.0 / HEAD_DIM)
        normed.append(d * lax.rsqrt(ms_h + LN_EPS))
    out = jnp.concatenate(normed, axis=0).T
    rows = pl.ds(pl.multiple_of(q_tile * DIFF_TQ, DIFF_TQ), DIFF_TQ)
    o_ref[0, rows, :] = (out * g_ref[...] * (1.0 - lambda_init)).astype(o_ref.dtype)


def _diff(ab, lam_vecs, subln_g, lambda_init):
    b, s, _ = ab.shape
    q_blk = 3 * NA_WIDTH // LANES
    k_blk = q_blk + DIFF_WIDTH // LANES
    v_blk = k_blk + DIFF_WIDTH // LANES
    g_tile = jnp.tile(subln_g.astype(F32), LANES // HEAD_DIM).reshape(1, LANES)
    return pl.pallas_call(
        functools.partial(_diff_kernel, seq=s, lambda_init=lambda_init),
        grid=(b, DIFF_HEADS // 2, s // DIFF_TQ),
        in_specs=[
            pl.BlockSpec((1, s, LANES), lambda bi, hp, i: (bi, 0, q_blk + hp)),
            pl.BlockSpec((1, s, LANES), lambda bi, hp, i: (bi, 0, k_blk + hp)),
            pl.BlockSpec((1, s, LANES), lambda bi, hp, i: (bi, 0, v_blk + hp)),
            pl.BlockSpec((4, DIFF_QK_DIM), lambda bi, hp, i: (0, 0)),
            pl.BlockSpec((1, LANES), lambda bi, hp, i: (0, 0)),
        ],
        out_specs=pl.BlockSpec((1, s, LANES), lambda bi, hp, i: (bi, 0, hp)),
        out_shape=jax.ShapeDtypeStruct((b, s, DIFF_WIDTH), BF16),
        scratch_shapes=[
            pltpu.VMEM((s // DIFF_TK, 2, DIFF_ACC_ROWS, DIFF_TK), BF16),
            pltpu.VMEM((4, DIFF_ACC_ROWS, DIFF_TQ), F32),
            pltpu.VMEM((4, DIFF_TK, DIFF_TQ), F32),
            pltpu.VMEM((4, DIFF_TK, DIFF_TQ), F32),
            pltpu.VMEM((4, LANES, DIFF_TQ), BF16),
        ],
        compiler_params=_params(("arbitrary", "arbitrary", "arbitrary")),
        name="diff",
    )(ab, ab, ab, lam_vecs.astype(F32), g_tile)


def _dil_kernel(q_ref, k_ref, v_ref, o_ref, o_scr, m_scr, l_scr, bias_scr, *, seq):
    tile = pl.program_id(2)
    hw = DIL_HALF_WIDTH
    lane = lax.broadcasted_iota(jnp.int32, (hw, LANES), 1)
    first = lane < HEAD_DIM
    row = lax.broadcasted_iota(jnp.int32, (2 * hw, DIL_KEY_LANES), 0) & (hw - 1)
    col = lax.broadcasted_iota(jnp.int32, (2 * hw, DIL_KEY_LANES), 1)
    pad = jnp.zeros((DIL_KEY_LANES - 3 * hw, LANES), BF16)
    delta = col - hw - row
    band = jnp.minimum(delta + hw, hw - delta)
    for variant in range(4):
        col_min = hw if variant & 1 else 0
        col_max = (2 * hw if variant & 2 else 3 * hw) - 1
        ok = jnp.minimum(band, jnp.minimum(col - col_min, col_max - col)) >= 0
        bias_scr[variant] = jnp.where(ok, 0.0, NEG_INF)

    for pat, (_, dil) in enumerate(DIL_PATTERNS):
        blocks_per_tile = DIL_TILE // (hw * dil)
        n_blocks = seq // (hw * dil)
        units = [(u % dil, u // dil) for u in range(DIL_TILE // hw)]
        for g0 in range(0, len(units), DIL_GROUP):
            group = units[g0:g0 + DIL_GROUP]
            scores, vcats = [], []
            for res, lb in group:
                blk = tile * blocks_per_tile + lb
                q = q_ref[0, pl.ds(lb * (hw * dil) + res, hw, stride=dil), :].astype(BF16)
                zero = jnp.zeros_like(q)
                q2 = jnp.concatenate([jnp.where(first, q, zero), jnp.where(first, zero, q)], axis=0)
                ks, vs = [], []
                for j in (-1, 0, 1):
                    kb = jnp.clip(blk + j, 0, n_blocks - 1)
                    k_start = kb * (hw * dil) + res
                    ks.append(k_ref[0, pl.ds(k_start, hw, stride=dil), :])
                    vs.append(v_ref[0, pl.ds(k_start, hw, stride=dil), :])
                kcat = jnp.concatenate([jnp.concatenate(ks, axis=0).astype(BF16), pad], axis=0)
                vcats.append(jnp.concatenate([jnp.concatenate(vs, axis=0).astype(BF16), pad], axis=0))
                s = lax.dot_general(q2, kcat, _NT_DIMS, preferred_element_type=F32)
                variant = jnp.where(blk > 0, 0, 1) + jnp.where(blk < n_blocks - 1, 0, 2)
                scores.append(s + bias_scr[variant])
            probs, stats = [], []
            for s in scores:
                m = jnp.max(s, axis=1, keepdims=True)
                p = jnp.exp2(s - m)
                stats.append((m, jnp.sum(p, axis=1, keepdims=True)))
                probs.append(p.astype(BF16))
            for (res, lb), p, vcat, (m, l) in zip(group, probs, vcats, stats):
                o2 = jnp.dot(p, vcat, preferred_element_type=F32)
                rows = pl.ds(lb * (hw * dil) + res, hw, stride=dil)
                o_scr[pat, rows, :] = jnp.where(first, o2[:hw], o2[hw:])
                m_scr[pat, rows, :] = jnp.where(first, m[:hw], m[hw:])
                l_scr[pat, rows, :] = jnp.where(first, l[:hw], l[hw:])

    m_all = jnp.maximum(jnp.maximum(m_scr[0], m_scr[1]), m_scr[2])
    num = jnp.zeros((DIL_TILE, LANES), F32)
    den = jnp.zeros((DIL_TILE, LANES), F32)
    for pat in range(len(DIL_PATTERNS)):
        w = jnp.exp2(m_scr[pat] - m_all)
        num = num + w * o_scr[pat]
        den = den + w * l_scr[pat]
    o_ref[0] = (num / den).astype(o_ref.dtype)


def _dil(c):
    b, s, _ = c.shape
    k_blk = DIL_WIDTH // LANES
    n_pat = len(DIL_PATTERNS)
    return pl.pallas_call(
        functools.partial(_dil_kernel, seq=s),
        grid=(b, DIL_HEADS // 2, s // DIL_TILE),
        in_specs=[
            pl.BlockSpec((1, DIL_TILE, LANES), lambda bi, hp, i: (bi, i, hp)),
            pl.BlockSpec((1, s, LANES), lambda bi, hp, i: (bi, 0, k_blk + hp)),
            pl.BlockSpec((1, s, LANES), lambda bi, hp, i: (bi, 0, 2 * k_blk + hp)),
        ],
        out_specs=pl.BlockSpec((1, DIL_TILE, LANES), lambda bi, hp, i: (bi, i, hp)),
        out_shape=jax.ShapeDtypeStruct((b, s, DIL_WIDTH), BF16),
        scratch_shapes=[
            pltpu.VMEM((n_pat, DIL_TILE, LANES), F32),
            pltpu.VMEM((n_pat, DIL_TILE, LANES), F32),
            pltpu.VMEM((n_pat, DIL_TILE, LANES), F32),
            pltpu.VMEM((4, 2 * DIL_HALF_WIDTH, DIL_KEY_LANES), F32),
        ],
        compiler_params=_params(("arbitrary", "arbitrary", "arbitrary")),
        name="dil",
    )(c, c, c)


def _layer_norm(z, g, b):
    mu = jnp.mean(z, axis=1, keepdims=True)
    zc = z - mu
    var = jnp.mean(zc * zc, axis=1, keepdims=True)
    return zc * lax.rsqrt(var + LN_EPS) * g + b


def _outproj_kernel(oa_ref, ob_ref, oc_ref, x_ref, mod_ref, w_ref, g_ref, b_ref, y_ref):
    g1 = mod_ref[0, 2:3, :]
    for sub in range(OUTPROJ_SUBTILES):
        rows = slice(sub * ROW_TILE, (sub + 1) * ROW_TILE)
        mix = jnp.dot(oa_ref[0, rows, :], w_ref[0:NA_WIDTH, :], preferred_element_type=F32)
        mix = mix + jnp.dot(ob_ref[0, rows, :], w_ref[NA_WIDTH:NA_WIDTH + DIFF_WIDTH, :],
                            preferred_element_type=F32)
        mix = mix + jnp.dot(oc_ref[0, rows, :], w_ref[NA_WIDTH + DIFF_WIDTH:MIX_WIDTH, :],
                            preferred_element_type=F32)
        z = DEEPNORM_ALPHA * x_ref[0, rows, :] + g1 * mix
        y_ref[0, rows, :] = _layer_norm(z, g_ref[...], b_ref[...])


def _outproj(oa, ob, oc, x, mod, w_out_bf16, ln_g, ln_b):
    b, s, _ = x.shape
    tm = OUTPROJ_SUBTILES * ROW_TILE
    row = lambda width: pl.BlockSpec((1, tm, width), lambda bi, si: (bi, si, 0))
    vec = pl.BlockSpec((1, D_MODEL), lambda bi, si: (0, 0))
    return pl.pallas_call(
        _outproj_kernel,
        grid=(b, s // tm),
        in_specs=[
            row(NA_WIDTH), row(DIFF_WIDTH), row(DIL_WIDTH), row(D_MODEL),
            pl.BlockSpec((1, 6, D_MODEL), lambda bi, si: (bi, 0, 0)),
            _resident((MIX_WIDTH, D_MODEL), lambda bi, si: (0, 0)),
            vec, vec,
        ],
        out_specs=row(D_MODEL),
        out_shape=jax.ShapeDtypeStruct((b, s, D_MODEL), F32),
        compiler_params=_params(("arbitrary", "arbitrary")),
        name="outproj",
    )(oa, ob, oc, x, mod, w_out_bf16, ln_g.reshape(1, D_MODEL), ln_b.reshape(1, D_MODEL))


def _ffn_kernel(x_ref, mod_ref, wgu_ref, wd_ref, g_ref, b_ref, y_ref):
    sh2 = mod_ref[0, 3:4, :]
    sc2 = mod_ref[0, 4:5, :]
    g2 = mod_ref[0, 5:6, :]
    for sub in range(FFN_SUBTILES):
        rows = slice(sub * ROW_TILE, (sub + 1) * ROW_TILE)
        x = x_ref[0, rows, :]
        h = (x * (1.0 + sc2) + sh2).astype(BF16)
        acc = jnp.zeros((ROW_TILE, D_MODEL), F32)
        for c0, c1 in FFN_CHUNKS:
            gate = jnp.dot(h, wgu_ref[:, c0:c1], preferred_element_type=F32)
            up = jnp.dot(h, wgu_ref[:, FFN_HIDDEN + c0:FFN_HIDDEN + c1], preferred_element_type=F32)
            act = (gate / (1.0 + jnp.exp(-gate)) * up).astype(BF16)
            acc = acc + jnp.dot(act, wd_ref[c0:c1, :], preferred_element_type=F32)
        z = DEEPNORM_ALPHA * x + g2 * acc
        y_ref[0, rows, :] = _layer_norm(z, g_ref[...], b_ref[...])


def _ffn(x, mod, w_gu_bf16, w_down_bf16, ln_g, ln_b):
    b, s, _ = x.shape
    tm = FFN_SUBTILES * ROW_TILE
    row = pl.BlockSpec((1, tm, D_MODEL), lambda bi, si: (bi, si, 0))
    vec = pl.BlockSpec((1, D_MODEL), lambda bi, si: (0, 0))
    return pl.pallas_call(
        _ffn_kernel,
        grid=(b, s // tm),
        in_specs=[
            row,
            pl.BlockSpec((1, 6, D_MODEL), lambda bi, si: (bi, 0, 0)),
            _resident((D_MODEL, 2 * FFN_HIDDEN), lambda bi, si: (0, 0)),
            _resident((FFN_HIDDEN, D_MODEL), lambda bi, si: (0, 0)),
            vec, vec,
        ],
        out_specs=row,
        out_shape=jax.ShapeDtypeStruct((b, s, D_MODEL), F32),
        compiler_params=_params(("arbitrary", "arbitrary")),
        name="ffn",
    )(x, mod, w_gu_bf16, w_down_bf16, ln_g.reshape(1, D_MODEL), ln_b.reshape(1, D_MODEL))


def _layer(x, mod, layer, tables, w_in_b, bias_tbl, diff_lambda, diff_subln_g, w_out_b,
           ln1_g, ln1_b, w_gu_b, w_down_b, ln2_g, ln2_b):
    lambda_init = 0.8 - 0.6 * math.exp(-0.3 * layer)
    ab, c = _inproj(x, mod, w_in_b, tables)
    oa = _na(ab, bias_tbl)
    ob = _diff(ab, diff_lambda, diff_subln_g, lambda_init)
    oc = _dil(c)
    x = _outproj(oa, ob, oc, x, mod, w_out_b, ln1_g, ln1_b)
    return _ffn(x, mod, w_gu_b, w_down_b, ln2_g, ln2_b)


def kernel(x_prompt, x_sample, c_prompt, c_sample, w_ada, b_ada, w_in, na_rpb, diff_lambda, diff_subln_g,
           w_out, ln1_g, ln1_b, w_gu, w_down, ln2_g, ln2_b):
    xs = [x_prompt, x_sample]
    n_prompt = c_prompt.shape[0]
    mod_all = _ada(jnp.concatenate([c_prompt, c_sample], axis=0), w_ada, b_ada)
    tables = [_rope_tables(x.shape[1], DIFF_QK_DIM) + _rope_tables(x.shape[1], HEAD_DIM) for x in xs]
    for layer in range(DEPTH):
        w_in_b = w_in[layer].astype(BF16)
        w_out_b = w_out[layer].astype(BF16)
        w_gu_b = w_gu[layer].astype(BF16)
        w_down_b = w_down[layer].astype(BF16)
        bias_tbl = _na_bias_table(na_rpb[layer])
        for g in range(2):
            nb = xs[g].shape[0]
            lo = 0 if g == 0 else n_prompt
            mod = mod_all[layer, lo:lo + nb].reshape(nb, 6, D_MODEL)
            xs[g] = _layer(xs[g], mod, layer, tables[g], w_in_b, bias_tbl, diff_lambda[layer],
                           diff_subln_g[layer], w_out_b, ln1_g[layer], ln1_b[layer], w_gu_b, w_down_b,
                           ln2_g[layer], ln2_b[layer])
    return tuple(xs)
```

```python
import functools
import math

import jax
import jax.numpy as jnp
from jax import lax
from jax.experimental import pallas as pl
from jax.experimental.pallas import tpu as pltpu

F32 = jnp.float32
BF16 = jnp.bfloat16

D_MODEL = 1024
DEPTH = 2
HEAD_DIM = 64
NA_HEADS = 4
DIFF_HEADS = 4
DIL_HEADS = 8
NA_WIDTH = NA_HEADS * HEAD_DIM
DIFF_WIDTH = DIFF_HEADS * HEAD_DIM
DIL_WIDTH = DIL_HEADS * HEAD_DIM
MIX_WIDTH = NA_WIDTH + DIFF_WIDTH + DIL_WIDTH
IN_WIDTH = 3 * MIX_WIDTH
AB_WIDTH = 3 * NA_WIDTH + 3 * DIFF_WIDTH
C_WIDTH = 3 * DIL_WIDTH
GRID_W = 64
NA_WIN_ROWS = 8
NA_WIN_COLS = 16
DIFF_QK_DIM = HEAD_DIM // 2
DIL_PATTERNS = ((128, 1), (512, 4), (2048, 16))
DIL_HALF_WIDTH = 64
FFN_HIDDEN = 2816
ROPE_THETA = 10000.0
LN_EPS = 1e-5
DEEPNORM_ALPHA = (2 * DEPTH) ** 0.25
NEG_INF = -1e30
LOG2E = math.log2(math.e)

LANES = 128
MXU_WIDTH = 256
VMEM_LIMIT_BYTES = 48 * 1024 * 1024

ROW_TILE = 512
INPROJ_ROW_TILE = 2 * ROW_TILE
OUTPROJ_SUBTILES = 2
FFN_SUBTILES = 2
NA_ROWS_PER_STEP = 32
DIFF_TQ = 256
DIFF_TK = 512
DIFF_KV_TILES_PER_STEP = 16
DIFF_ACC_ROWS = HEAD_DIM + 16
DIL_TILE = 1024
DIL_KEY_LANES = 2 * LANES
DIL_GROUP = 16
FFN_CHUNKS = ((0, 1024), (1024, 2048), (2048, FFN_HIDDEN))

_NT_DIMS = (((1,), (1,)), ((), ()))


def _params(semantics):
    return pltpu.CompilerParams(dimension_semantics=semantics, vmem_limit_bytes=VMEM_LIMIT_BYTES)


def _resident(block_shape, index_map):
    return pl.BlockSpec(block_shape, index_map, pipeline_mode=pl.Buffered(1))


def _ada_kernel(c_ref, w_ref, b_ref, o_ref):
    c = c_ref[...]
    a = c / (1.0 + jnp.exp(-c))
    o_ref[0] = jnp.dot(a, w_ref[0], precision=lax.Precision.HIGHEST,
                       preferred_element_type=F32) + b_ref[0]


def _ada(c_all, w_ada, b_ada):
    nb = c_all.shape[0]
    n_out = w_ada.shape[-1]
    tn = 1536
    return pl.pallas_call(
        _ada_kernel,
        grid=(DEPTH, n_out // tn),
        in_specs=[
            pl.BlockSpec((nb, D_MODEL), lambda l, n: (0, 0)),
            pl.BlockSpec((1, D_MODEL, tn), lambda l, n: (l, 0, n)),
            pl.BlockSpec((1, 1, tn), lambda l, n: (l, 0, n)),
        ],
        out_specs=pl.BlockSpec((1, nb, tn), lambda l, n: (l, 0, n)),
        out_shape=jax.ShapeDtypeStruct((DEPTH, nb, n_out), F32),
        compiler_params=_params(("arbitrary", "arbitrary")),
        name="ada",
    )(c_all, w_ada, b_ada.reshape(DEPTH, 1, n_out))


def _rope_tables(seq, dim):
    half = dim // 2
    inv_freq = ROPE_THETA ** (-jnp.arange(half, dtype=F32) / half)
    ang = jnp.arange(seq, dtype=F32)[:, None] * inv_freq[None, :]
    cos = jnp.cos(ang)
    sin = jnp.sin(ang)
    reps = LANES // dim
    cos_t = jnp.tile(jnp.concatenate([cos, cos], axis=1), (1, reps))
    sin_t = jnp.tile(jnp.concatenate([-sin, sin], axis=1), (1, reps))
    return cos_t, sin_t


def _rope(y, cos, sin_signed, half):
    lane = lax.broadcasted_iota(jnp.int32, y.shape, 1)
    first = (lane & (2 * half - 1)) < half
    partner = jnp.where(first, pltpu.roll(y, LANES - half, axis=1), pltpu.roll(y, half, axis=1))
    return y * cos + partner * sin_signed


_COLUMN_CLASSES = (
    (0, NA_WIDTH, None, HEAD_DIM ** -0.5),
    (NA_WIDTH, 3 * NA_WIDTH, None, None),
    (3 * NA_WIDTH, 3 * NA_WIDTH + DIFF_WIDTH, DIFF_QK_DIM, DIFF_QK_DIM ** -0.5 * LOG2E),
    (3 * NA_WIDTH + DIFF_WIDTH, 3 * NA_WIDTH + 2 * DIFF_WIDTH, DIFF_QK_DIM, None),
    (3 * NA_WIDTH + 2 * DIFF_WIDTH, AB_WIDTH, None, None),
    (AB_WIDTH, AB_WIDTH + DIL_WIDTH, HEAD_DIM, HEAD_DIM ** -0.5 * LOG2E),
    (AB_WIDTH + DIL_WIDTH, AB_WIDTH + 2 * DIL_WIDTH, HEAD_DIM, None),
    (AB_WIDTH + 2 * DIL_WIDTH, IN_WIDTH, None, None),
)


def _column_class(col):
    for lo, hi, rope_dim, scale in _COLUMN_CLASSES:
        if lo <= col < hi:
            return rope_dim, scale
    raise ValueError(col)


def _inproj_kernel(x_ref, mod_ref, w_ref, cos32_ref, sin32_ref, cos64_ref, sin64_ref, ab_ref, c_ref):
    sh1 = mod_ref[0, 0:1, :]
    sc1 = mod_ref[0, 1:2, :]
    h = (x_ref[0] * (1.0 + sc1) + sh1).astype(BF16)
    for n in range(IN_WIDTH // MXU_WIDTH):
        acc = jnp.dot(h, w_ref[:, n * MXU_WIDTH:(n + 1) * MXU_WIDTH], preferred_element_type=F32)
        for part in range(MXU_WIDTH // LANES):
            col = n * MXU_WIDTH + part * LANES
            y = acc[:, part * LANES:(part + 1) * LANES]
            rope_dim, scale = _column_class(col)
            if rope_dim == DIFF_QK_DIM:
                y = _rope(y, cos32_ref[...], sin32_ref[...], rope_dim // 2)
            elif rope_dim == HEAD_DIM:
                y = _rope(y, cos64_ref[...], sin64_ref[...], rope_dim // 2)
            if scale is not None:
                y = y * scale
            if col < AB_WIDTH:
                ab_ref[0, :, col:col + LANES] = y.astype(ab_ref.dtype)
            else:
                c_ref[0, :, col - AB_WIDTH:col - AB_WIDTH + LANES] = y


def _inproj(x, mod, w_in_bf16, tables):
    b, s, _ = x.shape
    tm = INPROJ_ROW_TILE
    table_spec = pl.BlockSpec((tm, LANES), lambda bi, si: (si, 0))
    return pl.pallas_call(
        _inproj_kernel,
        grid=(b, s // tm),
        in_specs=[
            pl.BlockSpec((1, tm, D_MODEL), lambda bi, si: (bi, si, 0)),
            pl.BlockSpec((1, 6, D_MODEL), lambda bi, si: (bi, 0, 0)),
            _resident((D_MODEL, IN_WIDTH), lambda bi, si: (0, 0)),
            table_spec, table_spec, table_spec, table_spec,
        ],
        out_specs=[
            pl.BlockSpec((1, tm, AB_WIDTH), lambda bi, si: (bi, si, 0)),
            pl.BlockSpec((1, tm, C_WIDTH), lambda bi, si: (bi, si, 0)),
        ],
        out_shape=[
            jax.ShapeDtypeStruct((b, s, AB_WIDTH), BF16),
            jax.ShapeDtypeStruct((b, s, C_WIDTH), F32),
        ],
        compiler_params=_params(("arbitrary", "arbitrary")),
        name="inproj",
    )(x, mod, w_in_bf16, *tables)


def _na_bias_table(rpb):
    c_idx = jnp.arange(GRID_W)
    c_start = jnp.clip(c_idx - NA_WIN_COLS // 2, 0, GRID_W - NA_WIN_COLS)
    col_in = (c_idx[None, :] >= c_start[:, None]) & (c_idx[None, :] < c_start[:, None] + NA_WIN_COLS)
    dc = jnp.clip(c_idx[None, :] - c_idx[:, None] + (NA_WIN_COLS - 1), 0, 2 * NA_WIN_COLS - 2)
    v_idx = jnp.arange(NA_WIN_ROWS)
    j_idx = jnp.arange(NA_WIN_ROWS)
    dr = j_idx[None, :] - v_idx[:, None] + (NA_WIN_ROWS - 1)
    bias = rpb.astype(F32)[:, dr][:, :, :, dc]
    bias = jnp.where(col_in[None, None, None], bias, NEG_INF)
    bias = bias.transpose(0, 1, 3, 2, 4)
    bias = bias.reshape(NA_HEADS // 2, 2, NA_WIN_ROWS, GRID_W, NA_WIN_ROWS * GRID_W)
    return bias.transpose(0, 2, 1, 3, 4).reshape(NA_HEADS // 2, NA_WIN_ROWS, 2 * GRID_W, NA_WIN_ROWS * GRID_W)


def _na_kernel(q_ref, k_ref, v_ref, bias_ref, o_ref, *, n_rows):
    step = pl.program_id(2)
    lane = lax.broadcasted_iota(jnp.int32, (GRID_W, LANES), 1)
    first = lane < HEAD_DIM
    win = NA_WIN_ROWS * GRID_W
    scores, vws = [], []
    for rr in range(NA_ROWS_PER_STEP):
        r = step * NA_ROWS_PER_STEP + rr
        row0 = jnp.clip(r - NA_WIN_ROWS // 2, 0, n_rows - NA_WIN_ROWS)
        start = pl.multiple_of(row0 * GRID_W, GRID_W)
        kw = k_ref[0, pl.ds(start, win), :]
        vws.append(v_ref[0, pl.ds(start, win), :])
        q = q_ref[0, rr * GRID_W:(rr + 1) * GRID_W, :]
        zero = jnp.zeros_like(q)
        q2 = jnp.concatenate([jnp.where(first, q, zero), jnp.where(first, zero, q)], axis=0)
        scores.append(lax.dot_general(q2, kw, _NT_DIMS, preferred_element_type=F32) + bias_ref[0, r - row0])
    probs, dens = [], []
    for s in scores:
        p = jnp.exp(s - jnp.max(s, axis=1, keepdims=True))
        dens.append(jnp.sum(p, axis=1, keepdims=True))
        probs.append(p.astype(BF16))
    for rr, (p, l, vw) in enumerate(zip(probs, dens, vws)):
        o2 = jnp.dot(p, vw, preferred_element_type=F32) / l
        o_ref[0, rr * GRID_W:(rr + 1) * GRID_W, :] = jnp.where(first, o2[:GRID_W], o2[GRID_W:]).astype(o_ref.dtype)


def _na(ab, bias_tbl):
    b, s, _ = ab.shape
    n_rows = s // GRID_W
    tq = NA_ROWS_PER_STEP * GRID_W
    k_blk = NA_WIDTH // LANES
    return pl.pallas_call(
        functools.partial(_na_kernel, n_rows=n_rows),
        grid=(b, NA_HEADS // 2, s // tq),
        in_specs=[
            pl.BlockSpec((1, tq, LANES), lambda bi, hp, i: (bi, i, hp)),
            pl.BlockSpec((1, s, LANES), lambda bi, hp, i: (bi, 0, k_blk + hp)),
            pl.BlockSpec((1, s, LANES), lambda bi, hp, i: (bi, 0, 2 * k_blk + hp)),
            pl.BlockSpec((1, NA_WIN_ROWS, 2 * GRID_W, NA_WIN_ROWS * GRID_W), lambda bi, hp, i: (hp, 0, 0, 0)),
        ],
        out_specs=pl.BlockSpec((1, tq, LANES), lambda bi, hp, i: (bi, i, hp)),
        out_shape=jax.ShapeDtypeStruct((b, s, NA_WIDTH), BF16),
        compiler_params=_params(("arbitrary", "arbitrary", "arbitrary")),
        name="na",
    )(ab, ab, ab, bias_tbl)


def _diff_kernel(q_ref, k_ref, v_ref, lam_ref, g_ref, o_ref, vt_scr, acc_scr, s_even, s_odd, qm_scr, *,
                 seq, lambda_init):
    n_kv = seq // DIFF_TK
    n_q = seq // DIFF_TQ
    n_sub = _diff_query_tiles_per_step(seq)
    assert n_kv % 2 == 0, "the last kv tile must read s_odd so that s_even is free for the next query tile"
    step = pl.program_id(2)
    feat = lax.broadcasted_iota(jnp.int32, (LANES, DIFF_TQ), 0)
    group = lax.shift_right_logical(feat, DIFF_QK_DIM.bit_length() - 1)

    def masked_qt(tile_idx):
        start = pl.multiple_of(tile_idx * DIFF_TQ, DIFF_TQ)
        qt = q_ref[0, pl.ds(start, DIFF_TQ), :].astype(F32).T
        return [jnp.where(group == c, qt, 0.0).astype(BF16) for c in range(4)]

    def scores(t, c, qm, dst):
        start = pl.multiple_of(t * DIFF_TK, DIFF_TK)
        dst[c] = jnp.dot(k_ref[0, pl.ds(start, DIFF_TK), :], qm, preferred_element_type=F32)

    @pl.when(step == 0)
    def _():
        ones = jnp.ones((DIFF_ACC_ROWS - HEAD_DIM, DIFF_TK), BF16)
        for t in range(n_kv):
            vt = v_ref[0, t * DIFF_TK:(t + 1) * DIFF_TK, :].astype(F32).T.astype(BF16)
            for h in range(2):
                vt_scr[t, h, 0:HEAD_DIM, :] = vt[h * HEAD_DIM:(h + 1) * HEAD_DIM]
                vt_scr[t, h, HEAD_DIM:DIFF_ACC_ROWS, :] = ones
        for c, qm in enumerate(masked_qt(0)):
            qm_scr[c] = qm
            scores(0, c, qm, s_even)

    lf = lam_ref[...]
    lam = (jnp.exp(jnp.sum(lf[0:1] * lf[1:2], axis=1, keepdims=True))
           - jnp.exp(jnp.sum(lf[2:3] * lf[3:4], axis=1, keepdims=True)) + lambda_init)

    qmts = [qm_scr[c] for c in range(4)]
    for sub in range(n_sub):
        q_tile = step * n_sub + sub
        acc_scr[...] = jnp.zeros(acc_scr.shape, F32)
        ms = tuple(jnp.full((1, DIFF_TQ), NEG_INF, F32) for _ in range(4))
        for t in range(n_kv):
            cur, nxt = (s_even, s_odd) if t % 2 == 0 else (s_odd, s_even)
            last = t + 1 == n_kv
            if last:
                next_qms = masked_qt(jnp.minimum(q_tile + 1, n_q - 1))
            new_ms = []
            for c in range(4):
                if last:
                    if sub + 1 == n_sub:
                        qm_scr[c] = next_qms[c]
                    scores(0, c, next_qms[c], nxt)
                else:
                    scores(t + 1, c, qmts[c], nxt)
                st = cur[c]
                m_new = jnp.maximum(ms[c], jnp.max(st, axis=0, keepdims=True))
                p = jnp.exp2(st - m_new).astype(BF16)
                alpha = jnp.exp2(ms[c] - m_new)
                acc_scr[c] = alpha * acc_scr[c] + jnp.dot(vt_scr[t, c // 2], p, preferred_element_type=F32)
                new_ms.append(m_new)
            ms = tuple(new_ms)
        qmts = next_qms

        o = [acc_scr[c, 0:HEAD_DIM, :] / acc_scr[c, HEAD_DIM:HEAD_DIM + 1, :] for c in range(4)]
        normed = []
        for h in range(2):
            d = o[2 * h] - lam * o[2 * h + 1]
            ms_h = jnp.sum(d * d, axis=0, keepdims=True) * (1.0 / HEAD_DIM)
            normed.append(d * lax.rsqrt(ms_h + LN_EPS))
        out = jnp.concatenate(normed, axis=0).T
        rows = pl.ds(pl.multiple_of(q_tile * DIFF_TQ, DIFF_TQ), DIFF_TQ)
        o_ref[0, rows, :] = (out * g_ref[...] * (1.0 - lambda_init)).astype(o_ref.dtype)


def _diff_query_tiles_per_step(seq):
    n_sub = max(1, DIFF_KV_TILES_PER_STEP // (seq // DIFF_TK))
    assert (seq // DIFF_TQ) % n_sub == 0
    return n_sub


def _diff(ab, lam_vecs, subln_g, lambda_init):
    b, s, _ = ab.shape
    q_blk = 3 * NA_WIDTH // LANES
    k_blk = q_blk + DIFF_WIDTH // LANES
    v_blk = k_blk + DIFF_WIDTH // LANES
    g_tile = jnp.tile(subln_g.astype(F32), LANES // HEAD_DIM).reshape(1, LANES)
    return pl.pallas_call(
        functools.partial(_diff_kernel, seq=s, lambda_init=lambda_init),
        grid=(b, DIFF_HEADS // 2, s // (DIFF_TQ * _diff_query_tiles_per_step(s))),
        in_specs=[
            pl.BlockSpec((1, s, LANES), lambda bi, hp, i: (bi, 0, q_blk + hp)),
            pl.BlockSpec((1, s, LANES), lambda bi, hp, i: (bi, 0, k_blk + hp)),
            pl.BlockSpec((1, s, LANES), lambda bi, hp, i: (bi, 0, v_blk + hp)),
            pl.BlockSpec((4, DIFF_QK_DIM), lambda bi, hp, i: (0, 0)),
            pl.BlockSpec((1, LANES), lambda bi, hp, i: (0, 0)),
        ],
        out_specs=pl.BlockSpec((1, s, LANES), lambda bi, hp, i: (bi, 0, hp)),
        out_shape=jax.ShapeDtypeStruct((b, s, DIFF_WIDTH), BF16),
        scratch_shapes=[
            pltpu.VMEM((s // DIFF_TK, 2, DIFF_ACC_ROWS, DIFF_TK), BF16),
            pltpu.VMEM((4, DIFF_ACC_ROWS, DIFF_TQ), F32),
            pltpu.VMEM((4, DIFF_TK, DIFF_TQ), F32),
            pltpu.VMEM((4, DIFF_TK, DIFF_TQ), F32),
            pltpu.VMEM((4, LANES, DIFF_TQ), BF16),
        ],
        compiler_params=_params(("arbitrary", "arbitrary", "arbitrary")),
        name="diff",
    )(ab, ab, ab, lam_vecs.astype(F32), g_tile)


def _dil_kernel(q_ref, k_ref, v_ref, o_ref, o_scr, m_scr, l_scr, bias_scr, *, seq):
    tile = pl.program_id(2)
    hw = DIL_HALF_WIDTH
    lane = lax.broadcasted_iota(jnp.int32, (hw, LANES), 1)
    first = lane < HEAD_DIM
    row = lax.broadcasted_iota(jnp.int32, (2 * hw, DIL_KEY_LANES), 0) & (hw - 1)
    col = lax.broadcasted_iota(jnp.int32, (2 * hw, DIL_KEY_LANES), 1)
    pad = jnp.zeros((DIL_KEY_LANES - 3 * hw, LANES), BF16)
    delta = col - hw - row
    band = jnp.minimum(delta + hw, hw - delta)
    for variant in range(4):
        col_min = hw if variant & 1 else 0
        col_max = (2 * hw if variant & 2 else 3 * hw) - 1
        ok = jnp.minimum(band, jnp.minimum(col - col_min, col_max - col)) >= 0
        bias_scr[variant] = jnp.where(ok, 0.0, NEG_INF)

    for pat, (_, dil) in enumerate(DIL_PATTERNS):
        blocks_per_tile = DIL_TILE // (hw * dil)
        n_blocks = seq // (hw * dil)
        units = [(u % dil, u // dil) for u in range(DIL_TILE // hw)]
        for g0 in range(0, len(units), DIL_GROUP):
            group = units[g0:g0 + DIL_GROUP]
            scores, vcats = [], []
            for res, lb in group:
                blk = tile * blocks_per_tile + lb
                q = q_ref[0, pl.ds(lb * (hw * dil) + res, hw, stride=dil), :].astype(BF16)
                zero = jnp.zeros_like(q)
                q2 = jnp.concatenate([jnp.where(first, q, zero), jnp.where(first, zero, q)], axis=0)
                ks, vs = [], []
                for j in (-1, 0, 1):
                    kb = jnp.clip(blk + j, 0, n_blocks - 1)
                    k_start = kb * (hw * dil) + res
                    ks.append(k_ref[0, pl.ds(k_start, hw, stride=dil), :])
                    vs.append(v_ref[0, pl.ds(k_start, hw, stride=dil), :])
                kcat = jnp.concatenate([jnp.concatenate(ks, axis=0).astype(BF16), pad], axis=0)
                vcats.append(jnp.concatenate([jnp.concatenate(vs, axis=0).astype(BF16), pad], axis=0))
                s = lax.dot_general(q2, kcat, _NT_DIMS, preferred_element_type=F32)
                variant = jnp.where(blk > 0, 0, 1) + jnp.where(blk < n_blocks - 1, 0, 2)
                scores.append(s + bias_scr[variant])
            probs, stats = [], []
            for s in scores:
                m = jnp.max(s, axis=1, keepdims=True)
                p = jnp.exp2(s - m)
                stats.append((m, jnp.sum(p, axis=1, keepdims=True)))
                probs.append(p.astype(BF16))
            for (res, lb), p, vcat, (m, l) in zip(group, probs, vcats, stats):
                o2 = jnp.dot(p, vcat, preferred_element_type=F32)
                rows = pl.ds(lb * (hw * dil) + res, hw, stride=dil)
                o_scr[pat, rows, :] = jnp.where(first, o2[:hw], o2[hw:])
                m_scr[pat, rows, :] = jnp.where(first, m[:hw], m[hw:])
                l_scr[pat, rows, :] = jnp.where(first, l[:hw], l[hw:])

    m_all = jnp.maximum(jnp.maximum(m_scr[0], m_scr[1]), m_scr[2])
    num = jnp.zeros((DIL_TILE, LANES), F32)
    den = jnp.zeros((DIL_TILE, LANES), F32)
    for pat in range(len(DIL_PATTERNS)):
        w = jnp.exp2(m_scr[pat] - m_all)
        num = num + w * o_scr[pat]
        den = den + w * l_scr[pat]
    o_ref[0] = (num / den).astype(o_ref.dtype)


def _dil(c):
    b, s, _ = c.shape
    k_blk = DIL_WIDTH // LANES
    n_pat = len(DIL_PATTERNS)
    return pl.pallas_call(
        functools.partial(_dil_kernel, seq=s),
        grid=(b, DIL_HEADS // 2, s // DIL_TILE),
        in_specs=[
            pl.BlockSpec((1, DIL_TILE, LANES), lambda bi, hp, i: (bi, i, hp)),
            pl.BlockSpec((1, s, LANES), lambda bi, hp, i: (bi, 0, k_blk + hp)),
            pl.BlockSpec((1, s, LANES), lambda bi, hp, i: (bi, 0, 2 * k_blk + hp)),
        ],
        out_specs=pl.BlockSpec((1, DIL_TILE, LANES), lambda bi, hp, i: (bi, i, hp)),
        out_shape=jax.ShapeDtypeStruct((b, s, DIL_WIDTH), BF16),
        scratch_shapes=[
            pltpu.VMEM((n_pat, DIL_TILE, LANES), F32),
            pltpu.VMEM((n_pat, DIL_TILE, LANES), F32),
            pltpu.VMEM((n_pat, DIL_TILE, LANES), F32),
            pltpu.VMEM((4, 2 * DIL_HALF_WIDTH, DIL_KEY_LANES), F32),
        ],
        compiler_params=_params(("arbitrary", "arbitrary", "arbitrary")),
        name="dil",
    )(c, c, c)


def _layer_norm(z, g, b):
    mu = jnp.mean(z, axis=1, keepdims=True)
    zc = z - mu
    var = jnp.mean(zc * zc, axis=1, keepdims=True)
    return zc * lax.rsqrt(var + LN_EPS) * g + b


def _outproj_kernel(oa_ref, ob_ref, oc_ref, x_ref, mod_ref, w_ref, g_ref, b_ref, y_ref):
    g1 = mod_ref[0, 2:3, :]
    for sub in range(OUTPROJ_SUBTILES):
        rows = slice(sub * ROW_TILE, (sub + 1) * ROW_TILE)
        mix = jnp.dot(oa_ref[0, rows, :], w_ref[0:NA_WIDTH, :], preferred_element_type=F32)
        mix = mix + jnp.dot(ob_ref[0, rows, :], w_ref[NA_WIDTH:NA_WIDTH + DIFF_WIDTH, :],
                            preferred_element_type=F32)
        mix = mix + jnp.dot(oc_ref[0, rows, :], w_ref[NA_WIDTH + DIFF_WIDTH:MIX_WIDTH, :],
                            preferred_element_type=F32)
        z = DEEPNORM_ALPHA * x_ref[0, rows, :] + g1 * mix
        y_ref[0, rows, :] = _layer_norm(z, g_ref[...], b_ref[...])


def _outproj(oa, ob, oc, x, mod, w_out_bf16, ln_g, ln_b):
    b, s, _ = x.shape
    tm = OUTPROJ_SUBTILES * ROW_TILE
    row = lambda width: pl.BlockSpec((1, tm, width), lambda bi, si: (bi, si, 0))
    vec = pl.BlockSpec((1, D_MODEL), lambda bi, si: (0, 0))
    return pl.pallas_call(
        _outproj_kernel,
        grid=(b, s // tm),
        in_specs=[
            row(NA_WIDTH), row(DIFF_WIDTH), row(DIL_WIDTH), row(D_MODEL),
            pl.BlockSpec((1, 6, D_MODEL), lambda bi, si: (bi, 0, 0)),
            _resident((MIX_WIDTH, D_MODEL), lambda bi, si: (0, 0)),
            vec, vec,
        ],
        out_specs=row(D_MODEL),
        out_shape=jax.ShapeDtypeStruct((b, s, D_MODEL), F32),
        compiler_params=_params(("arbitrary", "arbitrary")),
        name="outproj",
    )(oa, ob, oc, x, mod, w_out_bf16, ln_g.reshape(1, D_MODEL), ln_b.reshape(1, D_MODEL))


def _ffn_kernel(x_ref, mod_ref, wgu_ref, wd_ref, g_ref, b_ref, y_ref):
    sh2 = mod_ref[0, 3:4, :]
    sc2 = mod_ref[0, 4:5, :]
    g2 = mod_ref[0, 5:6, :]
    for sub in range(FFN_SUBTILES):
        rows = slice(sub * ROW_TILE, (sub + 1) * ROW_TILE)
        x = x_ref[0, rows, :]
        h = (x * (1.0 + sc2) + sh2).astype(BF16)
        acc = jnp.zeros((ROW_TILE, D_MODEL), F32)
        for c0, c1 in FFN_CHUNKS:
            gate = jnp.dot(h, wgu_ref[:, c0:c1], preferred_element_type=F32)
            up = jnp.dot(h, wgu_ref[:, FFN_HIDDEN + c0:FFN_HIDDEN + c1], preferred_element_type=F32)
            act = (gate / (1.0 + jnp.exp(-gate)) * up).astype(BF16)
            acc = acc + jnp.dot(act, wd_ref[c0:c1, :], preferred_element_type=F32)
        z = DEEPNORM_ALPHA * x + g2 * acc
        y_ref[0, rows, :] = _layer_norm(z, g_ref[...], b_ref[...])


def _ffn(x, mod, w_gu_bf16, w_down_bf16, ln_g, ln_b):
    b, s, _ = x.shape
    tm = FFN_SUBTILES * ROW_TILE
    row = pl.BlockSpec((1, tm, D_MODEL), lambda bi, si: (bi, si, 0))
    vec = pl.BlockSpec((1, D_MODEL), lambda bi, si: (0, 0))
    return pl.pallas_call(
        _ffn_kernel,
        grid=(b, s // tm),
        in_specs=[
            row,
            pl.BlockSpec((1, 6, D_MODEL), lambda bi, si: (bi, 0, 0)),
            _resident((D_MODEL, 2 * FFN_HIDDEN), lambda bi, si: (0, 0)),
            _resident((FFN_HIDDEN, D_MODEL), lambda bi, si: (0, 0)),
            vec, vec,
        ],
        out_specs=row,
        out_shape=jax.ShapeDtypeStruct((b, s, D_MODEL), F32),
        compiler_params=_params(("arbitrary", "arbitrary")),
        name="ffn",
    )(x, mod, w_gu_bf16, w_down_bf16, ln_g.reshape(1, D_MODEL), ln_b.reshape(1, D_MODEL))


def _layer(x, mod, layer, tables, w_in_b, bias_tbl, diff_lambda, diff_subln_g, w_out_b,
           ln1_g, ln1_b, w_gu_b, w_down_b, ln2_g, ln2_b):
    lambda_init = 0.8 - 0.6 * math.exp(-0.3 * layer)
    ab, c = _inproj(x, mod, w_in_b, tables)
    oa = _na(ab, bias_tbl)
    ob = _diff(ab, diff_lambda, diff_subln_g, lambda_init)
    oc = _dil(c)
    x = _outproj(oa, ob, oc, x, mod, w_out_b, ln1_g, ln1_b)
    return _ffn(x, mod, w_gu_b, w_down_b, ln2_g, ln2_b)


def kernel(x_prompt, x_sample, c_prompt, c_sample, w_ada, b_ada, w_in, na_rpb, diff_lambda, diff_subln_g,
           w_out, ln1_g, ln1_b, w_gu, w_down, ln2_g, ln2_b):
    xs = [x_prompt, x_sample]
    n_prompt = c_prompt.shape[0]
    mod_all = _ada(jnp.concatenate([c_prompt, c_sample], axis=0), w_ada, b_ada)
    tables = [_rope_tables(x.shape[1], DIFF_QK_DIM) + _rope_tables(x.shape[1], HEAD_DIM) for x in xs]
    for layer in range(DEPTH):
        w_in_b = w_in[layer].astype(BF16)
        w_out_b = w_out[layer].astype(BF16)
        w_gu_b = w_gu[layer].astype(BF16)
        w_down_b = w_down[layer].astype(BF16)
        bias_tbl = _na_bias_table(na_rpb[layer])
        for g in range(2):
            nb = xs[g].shape[0]
            lo = 0 if g == 0 else n_prompt
            mod = mod_all[layer, lo:lo + nb].reshape(nb, 6, D_MODEL)
            xs[g] = _layer(xs[g], mod, layer, tables[g], w_in_b, bias_tbl, diff_lambda[layer],
                           diff_subln_g[layer], w_out_b, ln1_g[layer], ln1_b[layer], w_gu_b, w_down_b,
                           ln2_g[layer], ln2_b[layer])
    return tuple(xs)
```

```python
import functools
import math

import jax
import jax.numpy as jnp
from jax import lax
from jax.experimental import pallas as pl
from jax.experimental.pallas import tpu as pltpu

F32 = jnp.float32
BF16 = jnp.bfloat16

D_MODEL = 1024
DEPTH = 2
HEAD_DIM = 64
NA_HEADS = 4
DIFF_HEADS = 4
DIL_HEADS = 8
NA_WIDTH = NA_HEADS * HEAD_DIM
DIFF_WIDTH = DIFF_HEADS * HEAD_DIM
DIL_WIDTH = DIL_HEADS * HEAD_DIM
MIX_WIDTH = NA_WIDTH + DIFF_WIDTH + DIL_WIDTH
IN_WIDTH = 3 * MIX_WIDTH
AB_WIDTH = 3 * NA_WIDTH + 3 * DIFF_WIDTH
C_WIDTH = 3 * DIL_WIDTH
GRID_W = 64
NA_WIN_ROWS = 8
NA_WIN_COLS = 16
DIFF_QK_DIM = HEAD_DIM // 2
DIL_PATTERNS = ((128, 1), (512, 4), (2048, 16))
DIL_HALF_WIDTH = 64
FFN_HIDDEN = 2816
ROPE_THETA = 10000.0
LN_EPS = 1e-5
DEEPNORM_ALPHA = (2 * DEPTH) ** 0.25
NEG_INF = -1e30
LOG2E = math.log2(math.e)

LANES = 128
MXU_WIDTH = 256
VMEM_LIMIT_BYTES = 48 * 1024 * 1024

ROW_TILE = 512
INPROJ_ROW_TILE = 2 * ROW_TILE
OUTPROJ_SUBTILES = 2
FFN_SUBTILES = 2
NA_ROWS_PER_STEP = 32
DIFF_TQ = 256
DIFF_TK = 512
DIFF_KV_TILES_PER_STEP = 32
DIFF_ACC_ROWS = HEAD_DIM + 16
DIL_TILE = 1024
DIL_KEY_LANES = 2 * LANES
DIL_GROUP = 16
FFN_CHUNKS = ((0, 1024), (1024, 2048), (2048, FFN_HIDDEN))

_NT_DIMS = (((1,), (1,)), ((), ()))


def _params(semantics):
    return pltpu.CompilerParams(dimension_semantics=semantics, vmem_limit_bytes=VMEM_LIMIT_BYTES)


def _resident(block_shape, index_map):
    return pl.BlockSpec(block_shape, index_map, pipeline_mode=pl.Buffered(1))


def _ada_kernel(c_ref, w_ref, b_ref, o_ref):
    c = c_ref[...]
    a = c / (1.0 + jnp.exp(-c))
    o_ref[0] = jnp.dot(a, w_ref[0], precision=lax.Precision.HIGHEST,
                       preferred_element_type=F32) + b_ref[0]


def _ada(c_all, w_ada, b_ada):
    nb = c_all.shape[0]
    n_out = w_ada.shape[-1]
    tn = 1536
    return pl.pallas_call(
        _ada_kernel,
        grid=(DEPTH, n_out // tn),
        in_specs=[
            pl.BlockSpec((nb, D_MODEL), lambda l, n: (0, 0)),
            pl.BlockSpec((1, D_MODEL, tn), lambda l, n: (l, 0, n)),
            pl.BlockSpec((1, 1, tn), lambda l, n: (l, 0, n)),
        ],
        out_specs=pl.BlockSpec((1, nb, tn), lambda l, n: (l, 0, n)),
        out_shape=jax.ShapeDtypeStruct((DEPTH, nb, n_out), F32),
        compiler_params=_params(("arbitrary", "arbitrary")),
        name="ada",
    )(c_all, w_ada, b_ada.reshape(DEPTH, 1, n_out))


def _rope_tables(seq, dim):
    half = dim // 2
    inv_freq = ROPE_THETA ** (-jnp.arange(half, dtype=F32) / half)
    ang = jnp.arange(seq, dtype=F32)[:, None] * inv_freq[None, :]
    cos = jnp.cos(ang)
    sin = jnp.sin(ang)
    reps = LANES // dim
    cos_t = jnp.tile(jnp.concatenate([cos, cos], axis=1), (1, reps))
    sin_t = jnp.tile(jnp.concatenate([-sin, sin], axis=1), (1, reps))
    return cos_t, sin_t


def _rope(y, cos, sin_signed, half):
    lane = lax.broadcasted_iota(jnp.int32, y.shape, 1)
    first = (lane & (2 * half - 1)) < half
    partner = jnp.where(first, pltpu.roll(y, LANES - half, axis=1), pltpu.roll(y, half, axis=1))
    return y * cos + partner * sin_signed


_COLUMN_CLASSES = (
    (0, NA_WIDTH, None, HEAD_DIM ** -0.5),
    (NA_WIDTH, 3 * NA_WIDTH, None, None),
    (3 * NA_WIDTH, 3 * NA_WIDTH + DIFF_WIDTH, DIFF_QK_DIM, DIFF_QK_DIM ** -0.5 * LOG2E),
    (3 * NA_WIDTH + DIFF_WIDTH, 3 * NA_WIDTH + 2 * DIFF_WIDTH, DIFF_QK_DIM, None),
    (3 * NA_WIDTH + 2 * DIFF_WIDTH, AB_WIDTH, None, None),
    (AB_WIDTH, AB_WIDTH + DIL_WIDTH, HEAD_DIM, HEAD_DIM ** -0.5 * LOG2E),
    (AB_WIDTH + DIL_WIDTH, AB_WIDTH + 2 * DIL_WIDTH, HEAD_DIM, None),
    (AB_WIDTH + 2 * DIL_WIDTH, IN_WIDTH, None, None),
)


def _column_class(col):
    for lo, hi, rope_dim, scale in _COLUMN_CLASSES:
        if lo <= col < hi:
            return rope_dim, scale
    raise ValueError(col)


def _inproj_kernel(x_ref, mod_ref, w_ref, cos32_ref, sin32_ref, cos64_ref, sin64_ref, ab_ref, c_ref):
    sh1 = mod_ref[0, 0:1, :]
    sc1 = mod_ref[0, 1:2, :]
    h = (x_ref[0] * (1.0 + sc1) + sh1).astype(BF16)
    for n in range(IN_WIDTH // MXU_WIDTH):
        acc = jnp.dot(h, w_ref[:, n * MXU_WIDTH:(n + 1) * MXU_WIDTH], preferred_element_type=F32)
        for part in range(MXU_WIDTH // LANES):
            col = n * MXU_WIDTH + part * LANES
            y = acc[:, part * LANES:(part + 1) * LANES]
            rope_dim, scale = _column_class(col)
            if rope_dim == DIFF_QK_DIM:
                y = _rope(y, cos32_ref[...], sin32_ref[...], rope_dim // 2)
            elif rope_dim == HEAD_DIM:
                y = _rope(y, cos64_ref[...], sin64_ref[...], rope_dim // 2)
            if scale is not None:
                y = y * scale
            if col < AB_WIDTH:
                ab_ref[0, :, col:col + LANES] = y.astype(ab_ref.dtype)
            else:
                c_ref[0, :, col - AB_WIDTH:col - AB_WIDTH + LANES] = y


def _inproj(x, mod, w_in_bf16, tables):
    b, s, _ = x.shape
    tm = INPROJ_ROW_TILE
    table_spec = pl.BlockSpec((tm, LANES), lambda bi, si: (si, 0))
    return pl.pallas_call(
        _inproj_kernel,
        grid=(b, s // tm),
        in_specs=[
            pl.BlockSpec((1, tm, D_MODEL), lambda bi, si: (bi, si, 0)),
            pl.BlockSpec((1, 6, D_MODEL), lambda bi, si: (bi, 0, 0)),
            _resident((D_MODEL, IN_WIDTH), lambda bi, si: (0, 0)),
            table_spec, table_spec, table_spec, table_spec,
        ],
        out_specs=[
            pl.BlockSpec((1, tm, AB_WIDTH), lambda bi, si: (bi, si, 0)),
            pl.BlockSpec((1, tm, C_WIDTH), lambda bi, si: (bi, si, 0)),
        ],
        out_shape=[
            jax.ShapeDtypeStruct((b, s, AB_WIDTH), BF16),
            jax.ShapeDtypeStruct((b, s, C_WIDTH), F32),
        ],
        compiler_params=_params(("arbitrary", "arbitrary")),
        name="inproj",
    )(x, mod, w_in_bf16, *tables)


def _na_bias_table(rpb):
    c_idx = jnp.arange(GRID_W)
    c_start = jnp.clip(c_idx - NA_WIN_COLS // 2, 0, GRID_W - NA_WIN_COLS)
    col_in = (c_idx[None, :] >= c_start[:, None]) & (c_idx[None, :] < c_start[:, None] + NA_WIN_COLS)
    dc = jnp.clip(c_idx[None, :] - c_idx[:, None] + (NA_WIN_COLS - 1), 0, 2 * NA_WIN_COLS - 2)
    v_idx = jnp.arange(NA_WIN_ROWS)
    j_idx = jnp.arange(NA_WIN_ROWS)
    dr = j_idx[None, :] - v_idx[:, None] + (NA_WIN_ROWS - 1)
    bias = rpb.astype(F32)[:, dr][:, :, :, dc]
    bias = jnp.where(col_in[None, None, None], bias, NEG_INF)
    bias = bias.transpose(0, 1, 3, 2, 4)
    bias = bias.reshape(NA_HEADS // 2, 2, NA_WIN_ROWS, GRID_W, NA_WIN_ROWS * GRID_W)
    return bias.transpose(0, 2, 1, 3, 4).reshape(NA_HEADS // 2, NA_WIN_ROWS, 2 * GRID_W, NA_WIN_ROWS * GRID_W)


def _na_kernel(q_ref, k_ref, v_ref, bias_ref, o_ref, *, n_rows):
    step = pl.program_id(2)
    lane = lax.broadcasted_iota(jnp.int32, (GRID_W, LANES), 1)
    first = lane < HEAD_DIM
    win = NA_WIN_ROWS * GRID_W
    scores, vws = [], []
    for rr in range(NA_ROWS_PER_STEP):
        r = step * NA_ROWS_PER_STEP + rr
        row0 = jnp.clip(r - NA_WIN_ROWS // 2, 0, n_rows - NA_WIN_ROWS)
        start = pl.multiple_of(row0 * GRID_W, GRID_W)
        kw = k_ref[0, pl.ds(start, win), :]
        vws.append(v_ref[0, pl.ds(start, win), :])
        q = q_ref[0, rr * GRID_W:(rr + 1) * GRID_W, :]
        zero = jnp.zeros_like(q)
        q2 = jnp.concatenate([jnp.where(first, q, zero), jnp.where(first, zero, q)], axis=0)
        scores.append(lax.dot_general(q2, kw, _NT_DIMS, preferred_element_type=F32) + bias_ref[0, r - row0])
    probs, dens = [], []
    for s in scores:
        p = jnp.exp(s - jnp.max(s, axis=1, keepdims=True))
        dens.append(jnp.sum(p, axis=1, keepdims=True))
        probs.append(p.astype(BF16))
    for rr, (p, l, vw) in enumerate(zip(probs, dens, vws)):
        o2 = jnp.dot(p, vw, preferred_element_type=F32) / l
        o_ref[0, rr * GRID_W:(rr + 1) * GRID_W, :] = jnp.where(first, o2[:GRID_W], o2[GRID_W:]).astype(o_ref.dtype)


def _na(ab, bias_tbl):
    b, s, _ = ab.shape
    n_rows = s // GRID_W
    tq = NA_ROWS_PER_STEP * GRID_W
    k_blk = NA_WIDTH // LANES
    return pl.pallas_call(
        functools.partial(_na_kernel, n_rows=n_rows),
        grid=(b, NA_HEADS // 2, s // tq),
        in_specs=[
            pl.BlockSpec((1, tq, LANES), lambda bi, hp, i: (bi, i, hp)),
            pl.BlockSpec((1, s, LANES), lambda bi, hp, i: (bi, 0, k_blk + hp)),
            pl.BlockSpec((1, s, LANES), lambda bi, hp, i: (bi, 0, 2 * k_blk + hp)),
            pl.BlockSpec((1, NA_WIN_ROWS, 2 * GRID_W, NA_WIN_ROWS * GRID_W), lambda bi, hp, i: (hp, 0, 0, 0)),
        ],
        out_specs=pl.BlockSpec((1, tq, LANES), lambda bi, hp, i: (bi, i, hp)),
        out_shape=jax.ShapeDtypeStruct((b, s, NA_WIDTH), BF16),
        compiler_params=_params(("arbitrary", "arbitrary", "arbitrary")),
        name="na",
    )(ab, ab, ab, bias_tbl)


def _diff_kernel(q_ref, k_ref, v_ref, lam_ref, g_ref, o_ref, vt_scr, acc_scr, s_even, s_odd, qm_scr, *,
                 seq, lambda_init):
    n_kv = seq // DIFF_TK
    n_q = seq // DIFF_TQ
    n_sub = _diff_query_tiles_per_step(seq)
    assert n_kv % 2 == 0, "the last kv tile must read s_odd so that s_even is free for the next query tile"
    step = pl.program_id(2)
    feat = lax.broadcasted_iota(jnp.int32, (LANES, DIFF_TQ), 0)
    group = lax.shift_right_logical(feat, DIFF_QK_DIM.bit_length() - 1)

    def masked_qt(tile_idx):
        start = pl.multiple_of(tile_idx * DIFF_TQ, DIFF_TQ)
        qt = q_ref[0, pl.ds(start, DIFF_TQ), :].astype(F32).T
        return [jnp.where(group == c, qt, 0.0).astype(BF16) for c in range(4)]

    def scores(t, c, qm, dst):
        start = pl.multiple_of(t * DIFF_TK, DIFF_TK)
        dst[c] = jnp.dot(k_ref[0, pl.ds(start, DIFF_TK), :], qm, preferred_element_type=F32)

    @pl.when(step == 0)
    def _():
        ones = jnp.ones((DIFF_ACC_ROWS - HEAD_DIM, DIFF_TK), BF16)
        for t in range(n_kv):
            vt = v_ref[0, t * DIFF_TK:(t + 1) * DIFF_TK, :].astype(F32).T.astype(BF16)
            for h in range(2):
                vt_scr[t, h, 0:HEAD_DIM, :] = vt[h * HEAD_DIM:(h + 1) * HEAD_DIM]
                vt_scr[t, h, HEAD_DIM:DIFF_ACC_ROWS, :] = ones
        for c, qm in enumerate(masked_qt(0)):
            qm_scr[c] = qm
            scores(0, c, qm, s_even)

    lf = lam_ref[...]
    lam = (jnp.exp(jnp.sum(lf[0:1] * lf[1:2], axis=1, keepdims=True))
           - jnp.exp(jnp.sum(lf[2:3] * lf[3:4], axis=1, keepdims=True)) + lambda_init)

    qmts = [qm_scr[c] for c in range(4)]
    for sub in range(n_sub):
        q_tile = step * n_sub + sub
        acc_scr[...] = jnp.zeros(acc_scr.shape, F32)
        ms = tuple(jnp.full((1, DIFF_TQ), NEG_INF, F32) for _ in range(4))
        for t in range(n_kv):
            cur, nxt = (s_even, s_odd) if t % 2 == 0 else (s_odd, s_even)
            last = t + 1 == n_kv
            if last:
                next_qms = masked_qt(jnp.minimum(q_tile + 1, n_q - 1))
            new_ms = []
            for c in range(4):
                if last:
                    if sub + 1 == n_sub:
                        qm_scr[c] = next_qms[c]
                    scores(0, c, next_qms[c], nxt)
                else:
                    scores(t + 1, c, qmts[c], nxt)
                st = cur[c]
                m_new = jnp.maximum(ms[c], jnp.max(st, axis=0, keepdims=True))
                p = jnp.exp2(st - m_new).astype(BF16)
                alpha = jnp.exp2(ms[c] - m_new)
                acc_scr[c] = alpha * acc_scr[c] + jnp.dot(vt_scr[t, c // 2], p, preferred_element_type=F32)
                new_ms.append(m_new)
            ms = tuple(new_ms)
        qmts = next_qms

        o = [acc_scr[c, 0:HEAD_DIM, :] / acc_scr[c, HEAD_DIM:HEAD_DIM + 1, :] for c in range(4)]
        normed = []
        for h in range(2):
            d = o[2 * h] - lam * o[2 * h + 1]
            ms_h = jnp.sum(d * d, axis=0, keepdims=True) * (1.0 / HEAD_DIM)
            normed.append(d * lax.rsqrt(ms_h + LN_EPS))
        out = jnp.concatenate(normed, axis=0).T
        rows = pl.ds(pl.multiple_of(q_tile * DIFF_TQ, DIFF_TQ), DIFF_TQ)
        o_ref[0, rows, :] = (out * g_ref[...] * (1.0 - lambda_init)).astype(o_ref.dtype)


def _diff_query_tiles_per_step(seq):
    n_sub = max(1, DIFF_KV_TILES_PER_STEP // (seq // DIFF_TK))
    assert (seq // DIFF_TQ) % n_sub == 0
    return n_sub


def _diff(ab, lam_vecs, subln_g, lambda_init):
    b, s, _ = ab.shape
    q_blk = 3 * NA_WIDTH // LANES
    k_blk = q_blk + DIFF_WIDTH // LANES
    v_blk = k_blk + DIFF_WIDTH // LANES
    g_tile = jnp.tile(subln_g.astype(F32), LANES // HEAD_DIM).reshape(1, LANES)
    return pl.pallas_call(
        functools.partial(_diff_kernel, seq=s, lambda_init=lambda_init),
        grid=(b, DIFF_HEADS // 2, s // (DIFF_TQ * _diff_query_tiles_per_step(s))),
        in_specs=[
            pl.BlockSpec((1, s, LANES), lambda bi, hp, i: (bi, 0, q_blk + hp)),
            pl.BlockSpec((1, s, LANES), lambda bi, hp, i: (bi, 0, k_blk + hp)),
            pl.BlockSpec((1, s, LANES), lambda bi, hp, i: (bi, 0, v_blk + hp)),
            pl.BlockSpec((4, DIFF_QK_DIM), lambda bi, hp, i: (0, 0)),
            pl.BlockSpec((1, LANES), lambda bi, hp, i: (0, 0)),
        ],
        out_specs=pl.BlockSpec((1, s, LANES), lambda bi, hp, i: (bi, 0, hp)),
        out_shape=jax.ShapeDtypeStruct((b, s, DIFF_WIDTH), BF16),
        scratch_shapes=[
            pltpu.VMEM((s // DIFF_TK, 2, DIFF_ACC_ROWS, DIFF_TK), BF16),
            pltpu.VMEM((4, DIFF_ACC_ROWS, DIFF_TQ), F32),
            pltpu.VMEM((4, DIFF_TK, DIFF_TQ), F32),
            pltpu.VMEM((4, DIFF_TK, DIFF_TQ), F32),
            pltpu.VMEM((4, LANES, DIFF_TQ), BF16),
        ],
        compiler_params=_params(("arbitrary", "arbitrary", "arbitrary")),
        name="diff",
    )(ab, ab, ab, lam_vecs.astype(F32), g_tile)


def _dil_kernel(q_ref, k_ref, v_ref, o_ref, o_scr, m_scr, l_scr, bias_scr, *, seq):
    tile = pl.program_id(2)
    hw = DIL_HALF_WIDTH
    lane = lax.broadcasted_iota(jnp.int32, (hw, LANES), 1)
    first = lane < HEAD_DIM
    row = lax.broadcasted_iota(jnp.int32, (2 * hw, DIL_KEY_LANES), 0) & (hw - 1)
    col = lax.broadcasted_iota(jnp.int32, (2 * hw, DIL_KEY_LANES), 1)
    pad = jnp.zeros((DIL_KEY_LANES - 3 * hw, LANES), BF16)
    delta = col - hw - row
    band = jnp.minimum(delta + hw, hw - delta)
    for variant in range(4):
        col_min = hw if variant & 1 else 0
        col_max = (2 * hw if variant & 2 else 3 * hw) - 1
        ok = jnp.minimum(band, jnp.minimum(col - col_min, col_max - col)) >= 0
        bias_scr[variant] = jnp.where(ok, 0.0, NEG_INF)

    for pat, (_, dil) in enumerate(DIL_PATTERNS):
        blocks_per_tile = DIL_TILE // (hw * dil)
        n_blocks = seq // (hw * dil)
        units = [(u % dil, u // dil) for u in range(DIL_TILE // hw)]
        for g0 in range(0, len(units), DIL_GROUP):
            group = units[g0:g0 + DIL_GROUP]
            scores, vcats = [], []
            for res, lb in group:
                blk = tile * blocks_per_tile + lb
                q = q_ref[0, pl.ds(lb * (hw * dil) + res, hw, stride=dil), :].astype(BF16)
                zero = jnp.zeros_like(q)
                q2 = jnp.concatenate([jnp.where(first, q, zero), jnp.where(first, zero, q)], axis=0)
                ks, vs = [], []
                for j in (-1, 0, 1):
                    kb = jnp.clip(blk + j, 0, n_blocks - 1)
                    k_start = kb * (hw * dil) + res
                    ks.append(k_ref[0, pl.ds(k_start, hw, stride=dil), :])
                    vs.append(v_ref[0, pl.ds(k_start, hw, stride=dil), :])
                kcat = jnp.concatenate([jnp.concatenate(ks, axis=0).astype(BF16), pad], axis=0)
                vcats.append(jnp.concatenate([jnp.concatenate(vs, axis=0).astype(BF16), pad], axis=0))
                s = lax.dot_general(q2, kcat, _NT_DIMS, preferred_element_type=F32)
                variant = jnp.where(blk > 0, 0, 1) + jnp.where(blk < n_blocks - 1, 0, 2)
                scores.append(s + bias_scr[variant])
            probs, stats = [], []
            for s in scores:
                m = jnp.max(s, axis=1, keepdims=True)
                p = jnp.exp2(s - m)
                stats.append((m, jnp.sum(p, axis=1, keepdims=True)))
                probs.append(p.astype(BF16))
            for (res, lb), p, vcat, (m, l) in zip(group, probs, vcats, stats):
                o2 = jnp.dot(p, vcat, preferred_element_type=F32)
                rows = pl.ds(lb * (hw * dil) + res, hw, stride=dil)
                o_scr[pat, rows, :] = jnp.where(first, o2[:hw], o2[hw:])
                m_scr[pat, rows, :] = jnp.where(first, m[:hw], m[hw:])
                l_scr[pat, rows, :] = jnp.where(first, l[:hw], l[hw:])

    m_all = jnp.maximum(jnp.maximum(m_scr[0], m_scr[1]), m_scr[2])
    num = jnp.zeros((DIL_TILE, LANES), F32)
    den = jnp.zeros((DIL_TILE, LANES), F32)
    for pat in range(len(DIL_PATTERNS)):
        w = jnp.exp2(m_scr[pat] - m_all)
        num = num + w * o_scr[pat]
        den = den + w * l_scr[pat]
    o_ref[0] = (num / den).astype(o_ref.dtype)


def _dil(c):
    b, s, _ = c.shape
    k_blk = DIL_WIDTH // LANES
    n_pat = len(DIL_PATTERNS)
    return pl.pallas_call(
        functools.partial(_dil_kernel, seq=s),
        grid=(b, DIL_HEADS // 2, s // DIL_TILE),
        in_specs=[
            pl.BlockSpec((1, DIL_TILE, LANES), lambda bi, hp, i: (bi, i, hp)),
            pl.BlockSpec((1, s, LANES), lambda bi, hp, i: (bi, 0, k_blk + hp)),
            pl.BlockSpec((1, s, LANES), lambda bi, hp, i: (bi, 0, 2 * k_blk + hp)),
        ],
        out_specs=pl.BlockSpec((1, DIL_TILE, LANES), lambda bi, hp, i: (bi, i, hp)),
        out_shape=jax.ShapeDtypeStruct((b, s, DIL_WIDTH), BF16),
        scratch_shapes=[
            pltpu.VMEM((n_pat, DIL_TILE, LANES), F32),
            pltpu.VMEM((n_pat, DIL_TILE, LANES), F32),
            pltpu.VMEM((n_pat, DIL_TILE, LANES), F32),
            pltpu.VMEM((4, 2 * DIL_HALF_WIDTH, DIL_KEY_LANES), F32),
        ],
        compiler_params=_params(("arbitrary", "arbitrary", "arbitrary")),
        name="dil",
    )(c, c, c)


def _layer_norm(z, g, b):
    mu = jnp.mean(z, axis=1, keepdims=True)
    zc = z - mu
    var = jnp.mean(zc * zc, axis=1, keepdims=True)
    return zc * lax.rsqrt(var + LN_EPS) * g + b


def _outproj_kernel(oa_ref, ob_ref, oc_ref, x_ref, mod_ref, w_ref, g_ref, b_ref, y_ref):
    g1 = mod_ref[0, 2:3, :]
    for sub in range(OUTPROJ_SUBTILES):
        rows = slice(sub * ROW_TILE, (sub + 1) * ROW_TILE)
        mix = jnp.dot(oa_ref[0, rows, :], w_ref[0:NA_WIDTH, :], preferred_element_type=F32)
        mix = mix + jnp.dot(ob_ref[0, rows, :], w_ref[NA_WIDTH:NA_WIDTH + DIFF_WIDTH, :],
                            preferred_element_type=F32)
        mix = mix + jnp.dot(oc_ref[0, rows, :], w_ref[NA_WIDTH + DIFF_WIDTH:MIX_WIDTH, :],
                            preferred_element_type=F32)
        z = DEEPNORM_ALPHA * x_ref[0, rows, :] + g1 * mix
        y_ref[0, rows, :] = _layer_norm(z, g_ref[...], b_ref[...])


def _outproj(oa, ob, oc, x, mod, w_out_bf16, ln_g, ln_b):
    b, s, _ = x.shape
    tm = OUTPROJ_SUBTILES * ROW_TILE
    row = lambda width: pl.BlockSpec((1, tm, width), lambda bi, si: (bi, si, 0))
    vec = pl.BlockSpec((1, D_MODEL), lambda bi, si: (0, 0))
    return pl.pallas_call(
        _outproj_kernel,
        grid=(b, s // tm),
        in_specs=[
            row(NA_WIDTH), row(DIFF_WIDTH), row(DIL_WIDTH), row(D_MODEL),
            pl.BlockSpec((1, 6, D_MODEL), lambda bi, si: (bi, 0, 0)),
            _resident((MIX_WIDTH, D_MODEL), lambda bi, si: (0, 0)),
            vec, vec,
        ],
        out_specs=row(D_MODEL),
        out_shape=jax.ShapeDtypeStruct((b, s, D_MODEL), F32),
        compiler_params=_params(("arbitrary", "arbitrary")),
        name="outproj",
    )(oa, ob, oc, x, mod, w_out_bf16, ln_g.reshape(1, D_MODEL), ln_b.reshape(1, D_MODEL))


def _ffn_kernel(x_ref, mod_ref, wgu_ref, wd_ref, g_ref, b_ref, y_ref):
    sh2 = mod_ref[0, 3:4, :]
    sc2 = mod_ref[0, 4:5, :]
    g2 = mod_ref[0, 5:6, :]
    for sub in range(FFN_SUBTILES):
        rows = slice(sub * ROW_TILE, (sub + 1) * ROW_TILE)
        x = x_ref[0, rows, :]
        h = (x * (1.0 + sc2) + sh2).astype(BF16)
        acc = jnp.zeros((ROW_TILE, D_MODEL), F32)
        for c0, c1 in FFN_CHUNKS:
            gate = jnp.dot(h, wgu_ref[:, c0:c1], preferred_element_type=F32)
            up = jnp.dot(h, wgu_ref[:, FFN_HIDDEN + c0:FFN_HIDDEN + c1], preferred_element_type=F32)
            act = (gate / (1.0 + jnp.exp(-gate)) * up).astype(BF16)
            acc = acc + jnp.dot(act, wd_ref[c0:c1, :], preferred_element_type=F32)
        z = DEEPNORM_ALPHA * x + g2 * acc
        y_ref[0, rows, :] = _layer_norm(z, g_ref[...], b_ref[...])


def _ffn(x, mod, w_gu_bf16, w_down_bf16, ln_g, ln_b):
    b, s, _ = x.shape
    tm = FFN_SUBTILES * ROW_TILE
    row = pl.BlockSpec((1, tm, D_MODEL), lambda bi, si: (bi, si, 0))
    vec = pl.BlockSpec((1, D_MODEL), lambda bi, si: (0, 0))
    return pl.pallas_call(
        _ffn_kernel,
        grid=(b, s // tm),
        in_specs=[
            row,
            pl.BlockSpec((1, 6, D_MODEL), lambda bi, si: (bi, 0, 0)),
            _resident((D_MODEL, 2 * FFN_HIDDEN), lambda bi, si: (0, 0)),
            _resident((FFN_HIDDEN, D_MODEL), lambda bi, si: (0, 0)),
            vec, vec,
        ],
        out_specs=row,
        out_shape=jax.ShapeDtypeStruct((b, s, D_MODEL), F32),
        compiler_params=_params(("arbitrary", "arbitrary")),
        name="ffn",
    )(x, mod, w_gu_bf16, w_down_bf16, ln_g.reshape(1, D_MODEL), ln_b.reshape(1, D_MODEL))


def _layer(x, mod, layer, tables, w_in_b, bias_tbl, diff_lambda, diff_subln_g, w_out_b,
           ln1_g, ln1_b, w_gu_b, w_down_b, ln2_g, ln2_b):
    lambda_init = 0.8 - 0.6 * math.exp(-0.3 * layer)
    ab, c = _inproj(x, mod, w_in_b, tables)
    oa = _na(ab, bias_tbl)
    ob = _diff(ab, diff_lambda, diff_subln_g, lambda_init)
    oc = _dil(c)
    x = _outproj(oa, ob, oc, x, mod, w_out_b, ln1_g, ln1_b)
    return _ffn(x, mod, w_gu_b, w_down_b, ln2_g, ln2_b)


def kernel(x_prompt, x_sample, c_prompt, c_sample, w_ada, b_ada, w_in, na_rpb, diff_lambda, diff_subln_g,
           w_out, ln1_g, ln1_b, w_gu, w_down, ln2_g, ln2_b):
    xs = [x_prompt, x_sample]
    n_prompt = c_prompt.shape[0]
    mod_all = _ada(jnp.concatenate([c_prompt, c_sample], axis=0), w_ada, b_ada)
    tables = [_rope_tables(x.shape[1], DIFF_QK_DIM) + _rope_tables(x.shape[1], HEAD_DIM) for x in xs]
    for layer in range(DEPTH):
        w_in_b = w_in[layer].astype(BF16)
        w_out_b = w_out[layer].astype(BF16)
        w_gu_b = w_gu[layer].astype(BF16)
        w_down_b = w_down[layer].astype(BF16)
        bias_tbl = _na_bias_table(na_rpb[layer])
        for g in range(2):
            nb = xs[g].shape[0]
            lo = 0 if g == 0 else n_prompt
            mod = mod_all[layer, lo:lo + nb].reshape(nb, 6, D_MODEL)
            xs[g] = _layer(xs[g], mod, layer, tables[g], w_in_b, bias_tbl, diff_lambda[layer],
                           diff_subln_g[layer], w_out_b, ln1_g[layer], ln1_b[layer], w_gu_b, w_down_b,
                           ln2_g[layer], ln2_b[layer])
    return tuple(xs)
```
